```python
import math
import jax, jax.numpy as jnp
from jax import lax
import numpy as np

D_MODEL = 2048
BATCH = 1
SEQ = 16384
DEPTH = 2

HEAD_DIM = D_MODEL // 16
A_HEADS = 6
A_KV_HEADS = 2
A_GROUP = A_HEADS // A_KV_HEADS
A_WINDOW = 128
B_HEADS = 6
B_PATTERNS = ((128, 1), (512, 4), (2048, 16))
M_HEADS = 4
MEM_LEN = 256
BLOCK = 128
REL_BUCKETS = 32
REL_MAX_EXACT = 16
REL_MAX_DISTANCE = 2048
D_FF = 5632
CONV_WIDTH = 3
EPS = 1e-6
NEG = -1e30

A_Q = A_HEADS * HEAD_DIM
A_KV = A_KV_HEADS * HEAD_DIM
B_W = B_HEADS * HEAD_DIM
M_W = M_HEADS * HEAD_DIM
MIX_WIDTH = A_Q + B_W + M_W
IN_COLS = A_Q + 2 * A_KV + 3 * B_W + M_W
REL_HEADS = A_HEADS + B_HEADS

kernel_name = "hybrid_swa_sink_dilated_mem_convffn"


def rms_norm(x, g):
    xf = x.astype(jnp.float32)
    y = xf * lax.rsqrt(jnp.mean(xf * xf, axis=-1, keepdims=True) + EPS)
    return (y * g.astype(jnp.float32)).astype(x.dtype)


def rel_bucket(dist):
    n = jnp.maximum(dist, 0)
    nf = jnp.maximum(n, 1).astype(jnp.float32)
    large = REL_MAX_EXACT + (jnp.log(nf / REL_MAX_EXACT)
                             / math.log(REL_MAX_DISTANCE / REL_MAX_EXACT)
                             * (REL_BUCKETS - REL_MAX_EXACT)).astype(jnp.int32)
    large = jnp.minimum(large, REL_BUCKETS - 1)
    return jnp.where(n < REL_MAX_EXACT, n, large)


def rel_bias_block(table, dilation, n_kv, n_group):
    rel = jnp.arange(BLOCK)[:, None] + BLOCK - jnp.arange(2 * BLOCK)[None, :]
    b = table[rel_bucket(rel * dilation)]
    return jnp.moveaxis(b, -1, 0).reshape(n_kv, n_group, BLOCK, 2 * BLOCK).astype(jnp.float32)


def banded_attention(q, k, v, bias, max_dist, sink=None):
    N, L, KV, G, Dh = q.shape
    nb = L // BLOCK
    qb = q.reshape(N, nb, BLOCK, KV, G, Dh)

    def two_blocks(t):
        tb = t.reshape(N, nb, BLOCK, KV, Dh)
        prev = jnp.pad(tb, ((0, 0), (1, 0), (0, 0), (0, 0), (0, 0)))[:, :-1]
        return jnp.concatenate([prev, tb], axis=2)

    kb, vb = two_blocks(k), two_blocks(v)
    s = jnp.einsum('nbqhgd,nbkhd->nbhgqk', qb, kb,
                   preferred_element_type=jnp.float32) * (Dh ** -0.5)
    rel = jnp.arange(BLOCK)[:, None] + BLOCK - jnp.arange(2 * BLOCK)[None, :]
    key_pos = jnp.arange(nb)[:, None] * BLOCK + jnp.arange(2 * BLOCK)[None, :] - BLOCK
    valid = ((rel >= 0) & (rel <= max_dist))[None] & (key_pos >= 0)[:, None, :]
    s = jnp.where(valid[None, :, None, None], s + bias[None, None], NEG)
    m = jnp.max(s, axis=-1)
    if sink is not None:
        sk = sink.astype(jnp.float32).reshape(KV, G)[None, None, :, :, None]
        m = jnp.maximum(m, sk)
    p = jnp.exp(s - m[..., None])
    den = jnp.sum(p, axis=-1)
    if sink is not None:
        den = den + jnp.exp(sk - m)
    o = jnp.einsum('nbhgqk,nbkhd->nbqhgd', p.astype(v.dtype), vb,
                   preferred_element_type=jnp.float32)
    o = o / jnp.transpose(den, (0, 1, 4, 2, 3))[..., None]
    lse = jnp.transpose(m + jnp.log(den), (0, 1, 4, 2, 3))
    return o.reshape(N, L, KV, G, Dh), lse.reshape(N, L, KV, G)


def mixer_a(q, k, v, sinks, bias_a):
    Bn, S = q.shape[:2]
    qg = q.reshape(Bn, S, A_KV_HEADS, A_GROUP, HEAD_DIM)
    o, _ = banded_attention(qg, k, v, bias_a, A_WINDOW - 1, sinks)
    return o.reshape(Bn, S, A_Q).astype(q.dtype)


def mixer_b(q, k, v, table_b):
    Bn, S = q.shape[:2]
    outs, lses = [], []
    for window, dil in B_PATTERNS:
        span = window // dil
        unit = dil * BLOCK
        S_pad = -(-S // unit) * unit
        M = S_pad // dil

        def by_class(t):
            t = jnp.pad(t, ((0, 0), (0, S_pad - S), (0, 0), (0, 0)))
            t = t.reshape(Bn, M, dil, B_HEADS, HEAD_DIM).transpose(0, 2, 1, 3, 4)
            return t.reshape(Bn * dil, M, B_HEADS, HEAD_DIM)

        qc, kc, vc = by_class(q), by_class(k), by_class(v)
        bias = rel_bias_block(table_b, dil, B_HEADS, 1)
        o, lse = banded_attention(qc[:, :, :, None], kc, vc, bias, span)
        o = o.reshape(Bn, dil, M, B_HEADS, HEAD_DIM).transpose(0, 2, 1, 3, 4)
        o = o.reshape(Bn, S_pad, B_HEADS, HEAD_DIM)[:, :S]
        lse = lse.reshape(Bn, dil, M, B_HEADS).transpose(0, 2, 1, 3).reshape(Bn, S_pad, B_HEADS)[:, :S]
        outs.append(o)
        lses.append(lse)
    w = jax.nn.softmax(jnp.stack(lses, axis=0), axis=0)
    o = jnp.sum(w[..., None] * jnp.stack(outs, axis=0), axis=0)
    return o.reshape(Bn, S, B_W).astype(q.dtype)


def mixer_mem(q, mk, mv):
    Bn, S = q.shape[:2]
    s = jnp.einsum('bshd,bmhd->bhsm', q, mk,
                   preferred_element_type=jnp.float32) * (HEAD_DIM ** -0.5)
    p = jax.nn.softmax(s, axis=-1)
    o = jnp.einsum('bhsm,bmhd->bshd', p.astype(mv.dtype), mv)
    return o.reshape(Bn, S, M_W)


def conv_glu_ffn(h, w_up, conv_w, conv_b, w_down):
    u = h @ w_up
    c = lax.conv_general_dilated(u, conv_w[:, None, :].astype(u.dtype), window_strides=(1,),
                                 padding=[(CONV_WIDTH - 1, 0)],
                                 dimension_numbers=('NWC', 'WIO', 'NWC'),
                                 feature_group_count=2 * D_FF) + conv_b
    gate, val = jnp.split(c, 2, axis=-1)
    return (jax.nn.gelu(gate, approximate=True) * val) @ w_down


def setup_inputs(seed: int = 0) -> dict:
    key = jax.random.key(seed)
    ks = jax.random.split(key, 20)
    f32 = jnp.float32

    def nrm(k, shape, scale):
        return jax.random.normal(k, shape, f32) * scale

    def gain(k, shape):
        return 1.0 + 0.05 * jax.random.normal(k, shape, f32)

    return {
        "x": nrm(ks[0], (BATCH, SEQ, D_MODEL), 1.0),
        "mem": nrm(ks[1], (BATCH, MEM_LEN, D_MODEL), 1.0),
        "rel_bias": nrm(ks[2], (REL_BUCKETS, REL_HEADS), 0.2),
        "g_mix_pre": gain(ks[3], (DEPTH, D_MODEL)),
        "w_in": nrm(ks[4], (DEPTH, D_MODEL, IN_COLS), D_MODEL ** -0.5),
        "a_sinks": nrm(ks[5], (DEPTH, A_HEADS), 0.5),
        "g_mem": gain(ks[6], (DEPTH, D_MODEL)),
        "w_mem_kv": nrm(ks[7], (DEPTH, D_MODEL, 2 * M_W), D_MODEL ** -0.5),
        "g_out": gain(ks[8], (DEPTH, MIX_WIDTH)),
        "w_out": nrm(ks[9], (DEPTH, MIX_WIDTH, D_MODEL), MIX_WIDTH ** -0.5),
        "g_mix_post": gain(ks[10], (DEPTH, D_MODEL)),
        "g_ffn_pre": gain(ks[11], (DEPTH, D_MODEL)),
        "w_up": nrm(ks[12], (DEPTH, D_MODEL, 2 * D_FF), D_MODEL ** -0.5),
        "conv_w": nrm(ks[13], (DEPTH, CONV_WIDTH, 2 * D_FF), CONV_WIDTH ** -0.5),
        "conv_b": nrm(ks[14], (DEPTH, 2 * D_FF), 0.01),
        "w_down": nrm(ks[15], (DEPTH, D_FF, D_MODEL), D_FF ** -0.5),
        "g_ffn_post": gain(ks[16], (DEPTH, D_MODEL)),
    }


def reference(x, mem, rel_bias, g_mix_pre, w_in, a_sinks, g_mem, w_mem_kv, g_out, w_out,
              g_mix_post, g_ffn_pre, w_up, conv_w, conv_b, w_down, g_ffn_post):
    Bn, S, _ = x.shape
    splits = [A_Q, A_Q + A_KV, A_Q + 2 * A_KV, A_Q + 2 * A_KV + B_W,
              A_Q + 2 * A_KV + 2 * B_W, A_Q + 2 * A_KV + 3 * B_W]
    bias_a = rel_bias_block(rel_bias[:, :A_HEADS], 1, A_KV_HEADS, A_GROUP)
    table_b = rel_bias[:, A_HEADS:]
    for l in range(DEPTH):
        h = rms_norm(x, g_mix_pre[l])
        proj = h @ w_in[l]
        qa, ka, va, qb, kb, vb, qm = jnp.split(proj, splits, axis=-1)
        qa = qa.reshape(Bn, S, A_HEADS, HEAD_DIM)
        ka = ka.reshape(Bn, S, A_KV_HEADS, HEAD_DIM)
        va = va.reshape(Bn, S, A_KV_HEADS, HEAD_DIM)
        qb = qb.reshape(Bn, S, B_HEADS, HEAD_DIM)
        kb = kb.reshape(Bn, S, B_HEADS, HEAD_DIM)
        vb = vb.reshape(Bn, S, B_HEADS, HEAD_DIM)
        qm = qm.reshape(Bn, S, M_HEADS, HEAD_DIM)
        mkv = rms_norm(mem, g_mem[l]) @ w_mem_kv[l]
        mk, mv = jnp.split(mkv, 2, axis=-1)
        mk = mk.reshape(Bn, MEM_LEN, M_HEADS, HEAD_DIM)
        mv = mv.reshape(Bn, MEM_LEN, M_HEADS, HEAD_DIM)

        oa = mixer_a(qa, ka, va, a_sinks[l], bias_a)
        ob = mixer_b(qb, kb, vb, table_b)
        om = mixer_mem(qm, mk, mv)
        y = jnp.concatenate([
            rms_norm(oa, g_out[l, :A_Q]),
            rms_norm(ob, g_out[l, A_Q:A_Q + B_W]),
            rms_norm(om, g_out[l, A_Q + B_W:]),
        ], axis=-1) @ w_out[l]
        x = x + rms_norm(y, g_mix_post[l])

        h = rms_norm(x, g_ffn_pre[l])
        f = conv_glu_ffn(h, w_up[l], conv_w[l], conv_b[l], w_down[l])
        x = x + rms_norm(f, g_ffn_post[l])
    return x
```

```python
import functools
import math

import jax
import jax.numpy as jnp
from jax import lax
from jax.experimental import pallas as pl
from jax.experimental.pallas import tpu as pltpu

F32 = jnp.float32
BF16 = jnp.bfloat16

HEAD_DIM = 128
BLOCK = 128
A_HEADS, A_KV_HEADS, A_GROUP = 6, 2, 3
A_WINDOW = 128
B_HEADS = 6
B_PATTERNS = ((128, 1), (512, 4), (2048, 16))
M_HEADS = 4
REL_BUCKETS, REL_MAX_EXACT, REL_MAX_DISTANCE = 32, 16, 2048
CONV_WIDTH = 3
EPS = 1e-6
NEG = -1e30

A_Q = A_HEADS * HEAD_DIM
A_KV = A_KV_HEADS * HEAD_DIM
B_W = B_HEADS * HEAD_DIM
M_W = M_HEADS * HEAD_DIM
Q_SCALE = HEAD_DIM ** -0.5

VMEM_LIMIT_BYTES = 56 * 1024 * 1024


def _params(n_axes):
    return pltpu.CompilerParams(dimension_semantics=("arbitrary",) * n_axes,
                                vmem_limit_bytes=VMEM_LIMIT_BYTES)


def _rms(x, g):
    return x * lax.rsqrt(jnp.mean(x * x, axis=-1, keepdims=True) + EPS) * g


def _dot_nt(a, b):
    return lax.dot_general(a, b, (((1,), (1,)), ((), ())), preferred_element_type=F32)


def _softmax_parts(s, sink=None):
    m = jnp.max(s, axis=-1, keepdims=True)
    if sink is not None:
        m = jnp.maximum(m, sink)
    p = jnp.exp(s - m)
    den = jnp.sum(p, axis=-1, keepdims=True)
    if sink is not None:
        den = den + jnp.exp(sink - m)
    return p, m, den


def _proj_kernel(x_ref, g_ref, w_ref, *o_refs, widths, scales):
    h = _rms(x_ref[...], g_ref[...]).astype(BF16)
    off = 0
    for o_ref, width, scale in zip(o_refs, widths, scales):
        acc = jnp.dot(h, w_ref[:, off:off + width], preferred_element_type=F32)
        if scale != 1.0:
            acc = acc * scale
        o_ref[...] = acc.astype(o_ref.dtype)
        off += width


def _norm_proj(x, g, w, widths, scales, tm):
    rows, d = x.shape
    n = w.shape[1]
    assert sum(widths) == n and rows % tm == 0
    return pl.pallas_call(
        functools.partial(_proj_kernel, widths=widths, scales=scales),
        grid=(rows // tm,),
        in_specs=[
            pl.BlockSpec((tm, d), lambda i: (i, 0)),
            pl.BlockSpec((1, d), lambda i: (0, 0)),
            pl.BlockSpec((d, n), lambda i: (0, 0), pipeline_mode=pl.Buffered(1)),
        ],
        out_specs=[pl.BlockSpec((tm, wd), lambda i: (i, 0)) for wd in widths],
        out_shape=[jax.ShapeDtypeStruct((rows, wd), BF16) for wd in widths],
        compiler_params=_params(1),
        name="norm_proj",
    )(x, g.reshape(1, d), w)


def _attn_am_kernel(sink_ref, qa_ref, ka_ref, kap_ref, va_ref, vap_ref, bfirst_ref, bias_ref,
                    qm_ref, mk_ref, mv_ref, ga_ref, gm_ref, ya_ref, ym_ref, *, nsub):
    for sb in range(nsub):
        rows = slice(sb * BLOCK, (sb + 1) * BLOCK)
        prev_rows = slice((sb - 1) * BLOCK, sb * BLOCK)
        b_ref = bfirst_ref if sb == 0 else bias_ref
        outs = []
        for kv in range(A_KV_HEADS):
            cols = slice(kv * HEAD_DIM, (kv + 1) * HEAD_DIM)
            if sb == 0:
                k_prev, v_prev = kap_ref[:, cols], vap_ref[:, cols]
            else:
                k_prev, v_prev = ka_ref[prev_rows, cols], va_ref[prev_rows, cols]
            kk = jnp.concatenate([k_prev, ka_ref[rows, cols]], axis=0)
            vv = jnp.concatenate([v_prev, va_ref[rows, cols]], axis=0)
            heads = [kv * A_GROUP + g for g in range(A_GROUP)]
            q3 = jnp.concatenate(
                [qa_ref[rows, h * HEAD_DIM:(h + 1) * HEAD_DIM] for h in heads], axis=0)
            s = _dot_nt(q3, kk) + b_ref[kv]
            sink = jnp.concatenate(
                [jnp.full((BLOCK, 1), sink_ref[h], F32) for h in heads], axis=0)
            p, _, den = _softmax_parts(s, sink)
            o = jnp.dot(p.astype(BF16), vv, preferred_element_type=F32) / den
            outs += [o[g * BLOCK:(g + 1) * BLOCK] for g in range(A_GROUP)]
        oa = jnp.concatenate(outs, axis=1)
        ya_ref[rows, :] = _rms(oa, ga_ref[...]).astype(BF16)

    outs = []
    for h in range(M_HEADS):
        cols = slice(h * HEAD_DIM, (h + 1) * HEAD_DIM)
        s = _dot_nt(qm_ref[:, cols], mk_ref[:, cols])
        p, _, den = _softmax_parts(s)
        outs.append(jnp.dot(p.astype(BF16), mv_ref[:, cols], preferred_element_type=F32) / den)
    ym_ref[...] = _rms(jnp.concatenate(outs, axis=1), gm_ref[...]).astype(BF16)


def _attn_am(qa, ka, va, qm, mk, mv, sinks, bias_a, g_a, g_m, tq):
    s_len = qa.shape[0]
    nsub = tq // BLOCK
    mem_len = mk.shape[0]
    prev = lambda i: (jnp.maximum(i * nsub - 1, 0), 0)
    return pl.pallas_call(
        functools.partial(_attn_am_kernel, nsub=nsub),
        grid=(s_len // tq,),
        in_specs=[
            pl.BlockSpec(memory_space=pltpu.SMEM),
            pl.BlockSpec((tq, A_Q), lambda i: (i, 0)),
            pl.BlockSpec((tq, A_KV), lambda i: (i, 0)),
            pl.BlockSpec((BLOCK, A_KV), prev),
            pl.BlockSpec((tq, A_KV), lambda i: (i, 0)),
            pl.BlockSpec((BLOCK, A_KV), prev),
            pl.BlockSpec((None, A_KV_HEADS, A_GROUP * BLOCK, 2 * BLOCK),
                         lambda i: (jnp.minimum(i, 1), 0, 0, 0)),
            pl.BlockSpec((None, A_KV_HEADS, A_GROUP * BLOCK, 2 * BLOCK), lambda i: (1, 0, 0, 0)),
            pl.BlockSpec((tq, M_W), lambda i: (i, 0)),
            pl.BlockSpec((mem_len, M_W), lambda i: (0, 0)),
            pl.BlockSpec((mem_len, M_W), lambda i: (0, 0)),
            pl.BlockSpec((1, A_Q), lambda i: (0, 0)),
            pl.BlockSpec((1, M_W), lambda i: (0, 0)),
        ],
        out_specs=[pl.BlockSpec((tq, A_Q), lambda i: (i, 0)),
                   pl.BlockSpec((tq, M_W), lambda i: (i, 0))],
        out_shape=[jax.ShapeDtypeStruct((s_len, A_Q), BF16),
                   jax.ShapeDtypeStruct((s_len, M_W), BF16)],
        compiler_params=_params(1),
        name="attn_am",
    )(sinks, qa, ka, ka, va, va, bias_a, bias_a, qm, mk, mv,
      g_a.reshape(1, A_Q), g_m.reshape(1, M_W))


def _attn_b_kernel(q_ref, k_ref, kp_ref, v_ref, vp_ref, bfirst_ref, bias_ref, o_ref, lse_ref,
                   *, nsub):
    lane = lax.broadcasted_iota(jnp.int32, (BLOCK, HEAD_DIM), 1)
    for sb in range(nsub):
        rows = slice(sb * BLOCK, (sb + 1) * BLOCK)
        prev_rows = slice((sb - 1) * BLOCK, sb * BLOCK)
        b_ref = bfirst_ref if sb == 0 else bias_ref
        lse_tile = jnp.zeros((BLOCK, HEAD_DIM), F32)
        for h in range(B_HEADS):
            cols = slice(h * HEAD_DIM, (h + 1) * HEAD_DIM)
            if sb == 0:
                k_prev, v_prev = kp_ref[:, cols], vp_ref[:, cols]
            else:
                k_prev, v_prev = k_ref[prev_rows, cols], v_ref[prev_rows, cols]
            kk = jnp.concatenate([k_prev, k_ref[rows, cols]], axis=0)
            vv = jnp.concatenate([v_prev, v_ref[rows, cols]], axis=0)
            s = _dot_nt(q_ref[rows, cols], kk) + b_ref[h]
            p, m, den = _softmax_parts(s)
            o_ref[rows, cols] = jnp.dot(p.astype(BF16), vv, preferred_element_type=F32) / den
            lse_tile = jnp.where(lane == h, m + jnp.log(den), lse_tile)
        lse_ref[rows, :] = lse_tile


def _attn_b(q, k, v, bias, dil, tq):
    s_len = q.shape[0]
    m_len = s_len // dil
    tq = min(tq, m_len)
    nsub = tq // BLOCK
    view = lambda t: t.reshape(m_len, dil * B_W)
    cur = lambda c, i: (i, c)
    prev = lambda c, i: (jnp.maximum(i * nsub - 1, 0), c)
    o, lse = pl.pallas_call(
        functools.partial(_attn_b_kernel, nsub=nsub),
        grid=(dil, m_len // tq),
        in_specs=[
            pl.BlockSpec((tq, B_W), cur),
            pl.BlockSpec((tq, B_W), cur),
            pl.BlockSpec((BLOCK, B_W), prev),
            pl.BlockSpec((tq, B_W), cur),
            pl.BlockSpec((BLOCK, B_W), prev),
            pl.BlockSpec((None, B_HEADS, BLOCK, 2 * BLOCK),
                         lambda c, i: (jnp.minimum(i, 1), 0, 0, 0)),
            pl.BlockSpec((None, B_HEADS, BLOCK, 2 * BLOCK), lambda c, i: (1, 0, 0, 0)),
        ],
        out_specs=[pl.BlockSpec((tq, B_W), cur), pl.BlockSpec((tq, HEAD_DIM), cur)],
        out_shape=[jax.ShapeDtypeStruct((m_len, dil * B_W), F32),
                   jax.ShapeDtypeStruct((m_len, dil * HEAD_DIM), F32)],
        compiler_params=_params(2),
        name=f"attn_b_d{dil}",
    )(view(q), view(k), view(k), view(v), view(v), bias, bias)
    return o.reshape(s_len, B_W), lse.reshape(s_len, HEAD_DIM)


def _out_kernel(x_ref, ya_ref, ym_ref, o1_ref, o2_ref, o3_ref, l1_ref, l2_ref, l3_ref,
                gb_ref, w_ref, gpost_ref, gpre_ref, x1_ref, h2_ref):
    o_refs = (o1_ref, o2_ref, o3_ref)
    obs = []
    for h in range(B_HEADS):
        cols = slice(h * HEAD_DIM, (h + 1) * HEAD_DIM)
        ls = [l_ref[:, h:h + 1] for l_ref in (l1_ref, l2_ref, l3_ref)]
        mx = jnp.maximum(jnp.maximum(ls[0], ls[1]), ls[2])
        es = [jnp.exp(l - mx) for l in ls]
        inv = 1.0 / (es[0] + es[1] + es[2])
        obs.append(sum((e * inv) * o_ref[:, cols] for e, o_ref in zip(es, o_refs)))
    yb = _rms(jnp.concatenate(obs, axis=1), gb_ref[...]).astype(BF16)
    y_in = jnp.concatenate([ya_ref[...], yb, ym_ref[...]], axis=1)
    y = jnp.dot(y_in, w_ref[...], preferred_element_type=F32)
    x1 = x_ref[...] + _rms(y, gpost_ref[...])
    x1_ref[...] = x1
    h2_ref[...] = _rms(x1, gpre_ref[...]).astype(BF16)


def _out_proj(x, ya, ym, obs, lses, g_b, w_out, g_post, g_pre, tm):
    s_len, d = x.shape
    row = lambda width: pl.BlockSpec((tm, width), lambda i: (i, 0))
    vec = lambda width: pl.BlockSpec((1, width), lambda i: (0, 0))
    return pl.pallas_call(
        _out_kernel,
        grid=(s_len // tm,),
        in_specs=[row(d), row(A_Q), row(M_W), row(B_W), row(B_W), row(B_W),
                  row(HEAD_DIM), row(HEAD_DIM), row(HEAD_DIM), vec(B_W),
                  pl.BlockSpec(w_out.shape, lambda i: (0, 0), pipeline_mode=pl.Buffered(1)),
                  vec(d), vec(d)],
        out_specs=[row(d), row(d)],
        out_shape=[jax.ShapeDtypeStruct((s_len, d), F32), jax.ShapeDtypeStruct((s_len, d), BF16)],
        compiler_params=_params(1),
        name="out_proj",
    )(x, ya, ym, *obs, *lses, g_b.reshape(1, B_W), w_out, g_post.reshape(1, d),
      g_pre.reshape(1, d))


def _shift_rows(u, carry, shift):
    rolled = pltpu.roll(u, shift, axis=0)
    head = jnp.where(lax.broadcasted_iota(jnp.int32, carry.shape, 0) < shift,
                     pltpu.roll(carry, shift, axis=0), rolled[:8])
    return jnp.concatenate([head, rolled[8:]], axis=0)


def _causal_conv(u, carry, cw, cb):
    return (cw[0:1] * _shift_rows(u, carry, 2) + cw[1:2] * _shift_rows(u, carry, 1)
            + cw[2:3] * u + cb)


def _ffn_kernel(h_ref, wg_ref, wv_ref, cwg_ref, cwv_ref, cbg_ref, cbv_ref, wd_ref, x_ref, g_ref,
                o_ref, carry_g, carry_v):
    i, j = pl.program_id(0), pl.program_id(1)
    h = h_ref[...]
    ug = jnp.dot(h, wg_ref[...], preferred_element_type=F32)
    uv = jnp.dot(h, wv_ref[...], preferred_element_type=F32)

    @pl.when(i == 0)
    def _():
        carry_g[j] = jnp.zeros(carry_g.shape[1:], F32)
        carry_v[j] = jnp.zeros(carry_v.shape[1:], F32)

    cg = _causal_conv(ug, carry_g[j], cwg_ref[...], cbg_ref[...])
    cv = _causal_conv(uv, carry_v[j], cwv_ref[...], cbv_ref[...])
    carry_g[j] = ug[-8:]
    carry_v[j] = uv[-8:]
    act = (jax.nn.gelu(cg, approximate=True) * cv).astype(BF16)
    part = jnp.dot(act, wd_ref[...], preferred_element_type=F32)

    @pl.when(j == 0)
    def _():
        o_ref[...] = part

    @pl.when(j > 0)
    def _():
        o_ref[...] += part

    @pl.when(j == pl.num_programs(1) - 1)
    def _():
        o_ref[...] = x_ref[...] + _rms(o_ref[...], g_ref[...])


def _ffn(h2, x1, w_up, conv_w, conv_b, w_down, g_post, tm, tf):
    s_len, d = x1.shape
    d_ff = w_down.shape[0]
    nf = d_ff // tf
    assert d_ff % tf == 0 and s_len % tm == 0
    return pl.pallas_call(
        _ffn_kernel,
        grid=(s_len // tm, nf),
        in_specs=[
            pl.BlockSpec((tm, d), lambda i, j: (i, 0)),
            pl.BlockSpec((d, tf), lambda i, j: (0, j)),
            pl.BlockSpec((d, tf), lambda i, j: (0, nf + j)),
            pl.BlockSpec((CONV_WIDTH, tf), lambda i, j: (0, j)),
            pl.BlockSpec((CONV_WIDTH, tf), lambda i, j: (0, nf + j)),
            pl.BlockSpec((1, tf), lambda i, j: (0, j)),
            pl.BlockSpec((1, tf), lambda i, j: (0, nf + j)),
            pl.BlockSpec((tf, d), lambda i, j: (j, 0)),
            pl.BlockSpec((tm, d), lambda i, j: (i, 0)),
            pl.BlockSpec((1, d), lambda i, j: (0, 0)),
        ],
        out_specs=pl.BlockSpec((tm, d), lambda i, j: (i, 0)),
        out_shape=jax.ShapeDtypeStruct((s_len, d), F32),
        scratch_shapes=[pltpu.VMEM((nf, 8, tf), F32), pltpu.VMEM((nf, 8, tf), F32)],
        compiler_params=_params(2),
        name="conv_glu_ffn",
    )(h2, w_up, w_up, conv_w, conv_w, conv_b.reshape(1, -1), conv_b.reshape(1, -1), w_down,
      x1, g_post.reshape(1, d))


def _rel_bucket(dist):
    n = jnp.maximum(dist, 0)
    nf = jnp.maximum(n, 1).astype(F32)
    large = REL_MAX_EXACT + (jnp.log(nf / REL_MAX_EXACT)
                             / math.log(REL_MAX_DISTANCE / REL_MAX_EXACT)
                             * (REL_BUCKETS - REL_MAX_EXACT)).astype(jnp.int32)
    large = jnp.minimum(large, REL_BUCKETS - 1)
    return jnp.where(n < REL_MAX_EXACT, n, large)


def _bias_blocks(table, dilation, max_dist):
    rel = jnp.arange(BLOCK)[:, None] + BLOCK - jnp.arange(2 * BLOCK)[None, :]
    b = jnp.moveaxis(table[_rel_bucket(rel * dilation)], -1, 0).astype(F32)
    valid = (rel >= 0) & (rel <= max_dist)
    normal = jnp.where(valid[None], b, NEG)
    first = jnp.where((jnp.arange(2 * BLOCK) >= BLOCK)[None, None, :], normal, NEG)
    return jnp.stack([first, normal])


def kernel(x, mem, rel_bias, g_mix_pre, w_in, a_sinks, g_mem, w_mem_kv, g_out, w_out, g_mix_post,
           g_ffn_pre, w_up, conv_w, conv_b, w_down, g_ffn_post):
    bn, s_len, d = x.shape
    assert bn == 1
    depth = w_in.shape[0]
    xs = x.reshape(s_len, d)
    mems = mem.reshape(mem.shape[1], d)

    bias_a = _bias_blocks(rel_bias[:, :A_HEADS], 1, A_WINDOW - 1)
    bias_a = bias_a.reshape(2, A_KV_HEADS, A_GROUP * BLOCK, 2 * BLOCK)
    bias_b = [_bias_blocks(rel_bias[:, A_HEADS:], dil, window // dil)
              for window, dil in B_PATTERNS]

    proj_widths = (A_Q, A_KV, A_KV, B_W, B_W, B_W, M_W)
    proj_scales = (Q_SCALE, 1.0, 1.0, Q_SCALE, 1.0, 1.0, Q_SCALE)

    for l in range(depth):
        qa, ka, va, qb, kb, vb, qm = _norm_proj(
            xs, g_mix_pre[l], w_in[l].astype(BF16), proj_widths, proj_scales, tm=512)
        mk, mv = _norm_proj(mems, g_mem[l], w_mem_kv[l].astype(BF16), (M_W, M_W), (1.0, 1.0),
                            tm=mems.shape[0])
        ya, ym = _attn_am(qa, ka, va, qm, mk, mv, a_sinks[l], bias_a,
                          g_out[l, :A_Q], g_out[l, A_Q + B_W:], tq=512)
        obs, lses = [], []
        for (window, dil), bias in zip(B_PATTERNS, bias_b):
            o, lse = _attn_b(qb, kb, vb, bias, dil, tq=512)
            obs.append(o)
            lses.append(lse)
        x1, h2 = _out_proj(xs, ya, ym, obs, lses, g_out[l, A_Q:A_Q + B_W], w_out[l].astype(BF16),
                           g_mix_post[l], g_ffn_pre[l], tm=512)
        xs = _ffn(h2, x1, w_up[l].astype(BF16), conv_w[l], conv_b[l], w_down[l].astype(BF16),
                  g_ffn_post[l], tm=512, tf=512)
    return xs.reshape(bn, s_len, d)
```

```python
import functools
import math

import jax
import jax.numpy as jnp
from jax import lax
from jax.experimental import pallas as pl
from jax.experimental.pallas import tpu as pltpu

F32 = jnp.float32
BF16 = jnp.bfloat16

HEAD_DIM = 128
BLOCK = 128
A_HEADS, A_KV_HEADS, A_GROUP = 6, 2, 3
A_WINDOW = 128
B_HEADS = 6
B_PATTERNS = ((128, 1), (512, 4), (2048, 16))
M_HEADS = 4
REL_BUCKETS, REL_MAX_EXACT, REL_MAX_DISTANCE = 32, 16, 2048
CONV_WIDTH = 3
EPS = 1e-6
NEG = -1e30

A_Q = A_HEADS * HEAD_DIM
A_KV = A_KV_HEADS * HEAD_DIM
B_W = B_HEADS * HEAD_DIM
M_W = M_HEADS * HEAD_DIM
Q_SCALE = HEAD_DIM ** -0.5

VMEM_LIMIT_BYTES = 56 * 1024 * 1024


def _params(n_axes):
    return pltpu.CompilerParams(dimension_semantics=("arbitrary",) * n_axes,
                                vmem_limit_bytes=VMEM_LIMIT_BYTES)


def _rms(x, g):
    return x * lax.rsqrt(jnp.mean(x * x, axis=-1, keepdims=True) + EPS) * g


def _dot_nt(a, b):
    return lax.dot_general(a, b, (((1,), (1,)), ((), ())), preferred_element_type=F32)


def _softmax_parts(s, sink=None):
    m = jnp.max(s, axis=-1, keepdims=True)
    if sink is not None:
        m = jnp.maximum(m, sink)
    p = jnp.exp(s - m)
    den = jnp.sum(p, axis=-1, keepdims=True)
    if sink is not None:
        den = den + jnp.exp(sink - m)
    return p, m, den


def _proj_kernel(x_ref, g_ref, w_ref, *o_refs, widths, scales):
    h = _rms(x_ref[...], g_ref[...]).astype(BF16)
    off = 0
    for o_ref, width, scale in zip(o_refs, widths, scales):
        acc = jnp.dot(h, w_ref[:, off:off + width], preferred_element_type=F32)
        if scale != 1.0:
            acc = acc * scale
        o_ref[...] = acc.astype(o_ref.dtype)
        off += width


def _norm_proj(x, g, w, widths, scales, tm):
    rows, d = x.shape
    n = w.shape[1]
    assert sum(widths) == n and rows % tm == 0
    return pl.pallas_call(
        functools.partial(_proj_kernel, widths=widths, scales=scales),
        grid=(rows // tm,),
        in_specs=[
            pl.BlockSpec((tm, d), lambda i: (i, 0)),
            pl.BlockSpec((1, d), lambda i: (0, 0)),
            pl.BlockSpec((d, n), lambda i: (0, 0), pipeline_mode=pl.Buffered(1)),
        ],
        out_specs=[pl.BlockSpec((tm, wd), lambda i: (i, 0)) for wd in widths],
        out_shape=[jax.ShapeDtypeStruct((rows, wd), BF16) for wd in widths],
        compiler_params=_params(1),
        name="norm_proj",
    )(x, g.reshape(1, d), w)


def _lookup_bias(table_ref, bucket, head):
    def body(b, acc):
        return jnp.where(bucket == b, table_ref[b, head], acc)
    return lax.fori_loop(0, REL_BUCKETS, body, jnp.full(bucket.shape, NEG, F32))


def _attn_am_kernel(sink_ref, table_ref, bucket_ref, qa_ref, ka_ref, kap_ref, va_ref, vap_ref,
                    qm_ref, mk_ref, mv_ref, ga_ref, gm_ref, ya_ref, ym_ref, bias_scr, *, nsub):
    i = pl.program_id(0)

    @pl.when(i == 0)
    def _():
        for var in range(2):
            for h in range(A_HEADS):
                kv, g = divmod(h, A_GROUP)
                bias_scr[var, kv, g * BLOCK:(g + 1) * BLOCK, :] = _lookup_bias(
                    table_ref, bucket_ref[var], h)

    for sb in range(nsub):
        rows = slice(sb * BLOCK, (sb + 1) * BLOCK)
        prev_rows = slice((sb - 1) * BLOCK, sb * BLOCK)
        b_ref = bias_scr.at[jnp.minimum(i, 1) if sb == 0 else 1]
        outs = []
        for kv in range(A_KV_HEADS):
            cols = slice(kv * HEAD_DIM, (kv + 1) * HEAD_DIM)
            if sb == 0:
                k_prev, v_prev = kap_ref[:, cols], vap_ref[:, cols]
            else:
                k_prev, v_prev = ka_ref[prev_rows, cols], va_ref[prev_rows, cols]
            kk = jnp.concatenate([k_prev, ka_ref[rows, cols]], axis=0)
            vv = jnp.concatenate([v_prev, va_ref[rows, cols]], axis=0)
            heads = [kv * A_GROUP + g for g in range(A_GROUP)]
            q3 = jnp.concatenate(
                [qa_ref[rows, h * HEAD_DIM:(h + 1) * HEAD_DIM] for h in heads], axis=0)
            s = _dot_nt(q3, kk) + b_ref[kv]
            sink = jnp.concatenate(
                [jnp.full((BLOCK, 1), sink_ref[h], F32) for h in heads], axis=0)
            p, _, den = _softmax_parts(s, sink)
            o = jnp.dot(p.astype(BF16), vv, preferred_element_type=F32) / den
            outs += [o[g * BLOCK:(g + 1) * BLOCK] for g in range(A_GROUP)]
        oa = jnp.concatenate(outs, axis=1)
        ya_ref[rows, :] = _rms(oa, ga_ref[...]).astype(BF16)

    outs = []
    for h in range(M_HEADS):
        cols = slice(h * HEAD_DIM, (h + 1) * HEAD_DIM)
        s = _dot_nt(qm_ref[:, cols], mk_ref[:, cols])
        p, _, den = _softmax_parts(s)
        outs.append(jnp.dot(p.astype(BF16), mv_ref[:, cols], preferred_element_type=F32) / den)
    ym_ref[...] = _rms(jnp.concatenate(outs, axis=1), gm_ref[...]).astype(BF16)


def _attn_am(qa, ka, va, qm, mk, mv, sinks, table, buckets, g_a, g_m, tq):
    s_len = qa.shape[0]
    nsub = tq // BLOCK
    mem_len = mk.shape[0]
    prev = lambda i: (jnp.maximum(i * nsub - 1, 0), 0)
    return pl.pallas_call(
        functools.partial(_attn_am_kernel, nsub=nsub),
        grid=(s_len // tq,),
        in_specs=[
            pl.BlockSpec(memory_space=pltpu.SMEM),
            pl.BlockSpec(memory_space=pltpu.SMEM),
            pl.BlockSpec(buckets.shape, lambda i: (0, 0, 0)),
            pl.BlockSpec((tq, A_Q), lambda i: (i, 0)),
            pl.BlockSpec((tq, A_KV), lambda i: (i, 0)),
            pl.BlockSpec((BLOCK, A_KV), prev),
            pl.BlockSpec((tq, A_KV), lambda i: (i, 0)),
            pl.BlockSpec((BLOCK, A_KV), prev),
            pl.BlockSpec((tq, M_W), lambda i: (i, 0)),
            pl.BlockSpec((mem_len, M_W), lambda i: (0, 0)),
            pl.BlockSpec((mem_len, M_W), lambda i: (0, 0)),
            pl.BlockSpec((1, A_Q), lambda i: (0, 0)),
            pl.BlockSpec((1, M_W), lambda i: (0, 0)),
        ],
        out_specs=[pl.BlockSpec((tq, A_Q), lambda i: (i, 0)),
                   pl.BlockSpec((tq, M_W), lambda i: (i, 0))],
        out_shape=[jax.ShapeDtypeStruct((s_len, A_Q), BF16),
                   jax.ShapeDtypeStruct((s_len, M_W), BF16)],
        scratch_shapes=[pltpu.VMEM((2, A_KV_HEADS, A_GROUP * BLOCK, 2 * BLOCK), F32)],
        compiler_params=_params(1),
        name="attn_am",
    )(sinks, table, buckets, qa, ka, ka, va, va, qm, mk, mv,
      g_a.reshape(1, A_Q), g_m.reshape(1, M_W))


def _attn_b_kernel(table_ref, bucket_ref, q_ref, k_ref, kp_ref, v_ref, vp_ref, o_ref, lse_ref,
                   bias_scr, *, nsub):
    c, i = pl.program_id(0), pl.program_id(1)

    @pl.when((c == 0) & (i == 0))
    def _():
        for var in range(2):
            for h in range(B_HEADS):
                bias_scr[var, h] = _lookup_bias(table_ref, bucket_ref[var], A_HEADS + h)

    lane = lax.broadcasted_iota(jnp.int32, (BLOCK, HEAD_DIM), 1)
    for sb in range(nsub):
        rows = slice(sb * BLOCK, (sb + 1) * BLOCK)
        prev_rows = slice((sb - 1) * BLOCK, sb * BLOCK)
        b_ref = bias_scr.at[jnp.minimum(i, 1) if sb == 0 else 1]
        lse_tile = jnp.zeros((BLOCK, HEAD_DIM), F32)
        for h in range(B_HEADS):
            cols = slice(h * HEAD_DIM, (h + 1) * HEAD_DIM)
            if sb == 0:
                k_prev, v_prev = kp_ref[:, cols], vp_ref[:, cols]
            else:
                k_prev, v_prev = k_ref[prev_rows, cols], v_ref[prev_rows, cols]
            kk = jnp.concatenate([k_prev, k_ref[rows, cols]], axis=0)
            vv = jnp.concatenate([v_prev, v_ref[rows, cols]], axis=0)
            s = _dot_nt(q_ref[rows, cols], kk) + b_ref[h]
            p, m, den = _softmax_parts(s)
            o_ref[rows, cols] = jnp.dot(p.astype(BF16), vv, preferred_element_type=F32) / den
            lse_tile = jnp.where(lane == h, m + jnp.log(den), lse_tile)
        lse_ref[rows, :] = lse_tile


def _attn_b(q, k, v, table, buckets, dil, tq):
    s_len = q.shape[0]
    m_len = s_len // dil
    tq = min(tq, m_len)
    nsub = tq // BLOCK
    view = lambda t: t.reshape(m_len, dil * B_W)
    cur = lambda c, i: (i, c)
    prev = lambda c, i: (jnp.maximum(i * nsub - 1, 0), c)
    o, lse = pl.pallas_call(
        functools.partial(_attn_b_kernel, nsub=nsub),
        grid=(dil, m_len // tq),
        in_specs=[
            pl.BlockSpec(memory_space=pltpu.SMEM),
            pl.BlockSpec(buckets.shape, lambda c, i: (0, 0, 0)),
            pl.BlockSpec((tq, B_W), cur),
            pl.BlockSpec((tq, B_W), cur),
            pl.BlockSpec((BLOCK, B_W), prev),
            pl.BlockSpec((tq, B_W), cur),
            pl.BlockSpec((BLOCK, B_W), prev),
        ],
        out_specs=[pl.BlockSpec((tq, B_W), cur), pl.BlockSpec((tq, HEAD_DIM), cur)],
        out_shape=[jax.ShapeDtypeStruct((m_len, dil * B_W), F32),
                   jax.ShapeDtypeStruct((m_len, dil * HEAD_DIM), F32)],
        scratch_shapes=[pltpu.VMEM((2, B_HEADS, BLOCK, 2 * BLOCK), F32)],
        compiler_params=_params(2),
        name=f"attn_b_d{dil}",
    )(table, buckets, view(q), view(k), view(k), view(v), view(v))
    return o.reshape(s_len, B_W), lse.reshape(s_len, HEAD_DIM)


def _out_kernel(x_ref, ya_ref, ym_ref, o1_ref, o2_ref, o3_ref, l1_ref, l2_ref, l3_ref,
                gb_ref, w_ref, gpost_ref, gpre_ref, x1_ref, h2_ref):
    o_refs = (o1_ref, o2_ref, o3_ref)
    obs = []
    for h in range(B_HEADS):
        cols = slice(h * HEAD_DIM, (h + 1) * HEAD_DIM)
        ls = [l_ref[:, h:h + 1] for l_ref in (l1_ref, l2_ref, l3_ref)]
        mx = jnp.maximum(jnp.maximum(ls[0], ls[1]), ls[2])
        es = [jnp.exp(l - mx) for l in ls]
        inv = 1.0 / (es[0] + es[1] + es[2])
        obs.append(sum((e * inv) * o_ref[:, cols] for e, o_ref in zip(es, o_refs)))
    yb = _rms(jnp.concatenate(obs, axis=1), gb_ref[...]).astype(BF16)
    y_in = jnp.concatenate([ya_ref[...], yb, ym_ref[...]], axis=1)
    y = jnp.dot(y_in, w_ref[...], preferred_element_type=F32)
    x1 = x_ref[...] + _rms(y, gpost_ref[...])
    x1_ref[...] = x1
    h2_ref[...] = _rms(x1, gpre_ref[...]).astype(BF16)


def _out_proj(x, ya, ym, obs, lses, g_b, w_out, g_post, g_pre, tm):
    s_len, d = x.shape
    row = lambda width: pl.BlockSpec((tm, width), lambda i: (i, 0))
    vec = lambda width: pl.BlockSpec((1, width), lambda i: (0, 0))
    return pl.pallas_call(
        _out_kernel,
        grid=(s_len // tm,),
        in_specs=[row(d), row(A_Q), row(M_W), row(B_W), row(B_W), row(B_W),
                  row(HEAD_DIM), row(HEAD_DIM), row(HEAD_DIM), vec(B_W),
                  pl.BlockSpec(w_out.shape, lambda i: (0, 0), pipeline_mode=pl.Buffered(1)),
                  vec(d), vec(d)],
        out_specs=[row(d), row(d)],
        out_shape=[jax.ShapeDtypeStruct((s_len, d), F32), jax.ShapeDtypeStruct((s_len, d), BF16)],
        compiler_params=_params(1),
        name="out_proj",
    )(x, ya, ym, *obs, *lses, g_b.reshape(1, B_W), w_out, g_post.reshape(1, d),
      g_pre.reshape(1, d))


def _shift_rows(u, carry, shift):
    rolled = pltpu.roll(u, shift, axis=0)
    head = jnp.where(lax.broadcasted_iota(jnp.int32, carry.shape, 0) < shift,
                     pltpu.roll(carry, shift, axis=0), rolled[:8])
    return jnp.concatenate([head, rolled[8:]], axis=0)


def _causal_conv(u, carry, cw, cb):
    return (cw[0:1] * _shift_rows(u, carry, 2) + cw[1:2] * _shift_rows(u, carry, 1)
            + cw[2:3] * u + cb)


def _ffn_kernel(h_ref, wg_ref, wv_ref, cwg_ref, cwv_ref, cbg_ref, cbv_ref, wd_ref, x_ref, g_ref,
                o_ref, carry_g, carry_v, *, chunk):
    i, j = pl.program_id(0), pl.program_id(1)

    @pl.when(i == 0)
    def _():
        carry_g[j] = jnp.zeros(carry_g.shape[1:], F32)
        carry_v[j] = jnp.zeros(carry_v.shape[1:], F32)

    @pl.when(j == 0)
    def _():
        o_ref[...] = jnp.zeros(o_ref.shape, F32)

    prev_g, prev_v = carry_g[j], carry_v[j]
    for c in range(h_ref.shape[0] // chunk):
        rows = slice(c * chunk, (c + 1) * chunk)
        h = h_ref[rows, :]
        ug = jnp.dot(h, wg_ref[...], preferred_element_type=F32)
        uv = jnp.dot(h, wv_ref[...], preferred_element_type=F32)
        cg = _causal_conv(ug, prev_g, cwg_ref[...], cbg_ref[...])
        cv = _causal_conv(uv, prev_v, cwv_ref[...], cbv_ref[...])
        prev_g, prev_v = ug[-8:], uv[-8:]
        act = (jax.nn.gelu(cg, approximate=True) * cv).astype(BF16)
        o_ref[rows, :] += jnp.dot(act, wd_ref[...], preferred_element_type=F32)
    carry_g[j] = prev_g
    carry_v[j] = prev_v

    @pl.when(j == pl.num_programs(1) - 1)
    def _():
        o_ref[...] = x_ref[...] + _rms(o_ref[...], g_ref[...])


def _ffn(h2, x1, w_up, conv_w, conv_b, w_down, g_post, tm, tf, chunk):
    s_len, d = x1.shape
    d_ff = w_down.shape[0]
    nf = d_ff // tf
    assert d_ff % tf == 0 and s_len % tm == 0 and tm % chunk == 0
    return pl.pallas_call(
        functools.partial(_ffn_kernel, chunk=chunk),
        grid=(s_len // tm, nf),
        in_specs=[
            pl.BlockSpec((tm, d), lambda i, j: (i, 0)),
            pl.BlockSpec((d, tf), lambda i, j: (0, j)),
            pl.BlockSpec((d, tf), lambda i, j: (0, nf + j)),
            pl.BlockSpec((CONV_WIDTH, tf), lambda i, j: (0, j)),
            pl.BlockSpec((CONV_WIDTH, tf), lambda i, j: (0, nf + j)),
            pl.BlockSpec((1, tf), lambda i, j: (0, j)),
            pl.BlockSpec((1, tf), lambda i, j: (0, nf + j)),
            pl.BlockSpec((tf, d), lambda i, j: (j, 0)),
            pl.BlockSpec((tm, d), lambda i, j: (i, 0), pipeline_mode=pl.Buffered(1)),
            pl.BlockSpec((1, d), lambda i, j: (0, 0)),
        ],
        out_specs=pl.BlockSpec((tm, d), lambda i, j: (i, 0)),
        out_shape=jax.ShapeDtypeStruct((s_len, d), F32),
        scratch_shapes=[pltpu.VMEM((nf, 8, tf), F32), pltpu.VMEM((nf, 8, tf), F32)],
        compiler_params=_params(2),
        name="conv_glu_ffn",
    )(h2, w_up, w_up, conv_w, conv_w, conv_b.reshape(1, -1), conv_b.reshape(1, -1), w_down,
      x1, g_post.reshape(1, d))


def _rel_bucket(dist):
    n = jnp.maximum(dist, 0)
    nf = jnp.maximum(n, 1).astype(F32)
    large = REL_MAX_EXACT + (jnp.log(nf / REL_MAX_EXACT)
                             / math.log(REL_MAX_DISTANCE / REL_MAX_EXACT)
                             * (REL_BUCKETS - REL_MAX_EXACT)).astype(jnp.int32)
    large = jnp.minimum(large, REL_BUCKETS - 1)
    return jnp.where(n < REL_MAX_EXACT, n, large)


def _bucket_blocks(dilation, max_dist):
    rel = jnp.arange(BLOCK)[:, None] + BLOCK - jnp.arange(2 * BLOCK)[None, :]
    valid = (rel >= 0) & (rel <= max_dist)
    normal = jnp.where(valid, _rel_bucket(rel * dilation), -1)
    first = jnp.where((jnp.arange(2 * BLOCK) >= BLOCK)[None, :], normal, -1)
    return jnp.stack([first, normal]).astype(jnp.int32)


def kernel(x, mem, rel_bias, g_mix_pre, w_in, a_sinks, g_mem, w_mem_kv, g_out, w_out, g_mix_post,
           g_ffn_pre, w_up, conv_w, conv_b, w_down, g_ffn_post):
    bn, s_len, d = x.shape
    assert bn == 1
    depth = w_in.shape[0]
    xs = x.reshape(s_len, d)
    mems = mem.reshape(mem.shape[1], d)

    buckets_a = _bucket_blocks(1, A_WINDOW - 1)
    buckets_b = [_bucket_blocks(dil, window // dil) for window, dil in B_PATTERNS]

    proj_widths = (A_Q, A_KV, A_KV, B_W, B_W, B_W, M_W)
    proj_scales = (Q_SCALE, 1.0, 1.0, Q_SCALE, 1.0, 1.0, Q_SCALE)

    for l in range(depth):
        qa, ka, va, qb, kb, vb, qm = _norm_proj(
            xs, g_mix_pre[l], w_in[l].astype(BF16), proj_widths, proj_scales, tm=512)
        mk, mv = _norm_proj(mems, g_mem[l], w_mem_kv[l].astype(BF16), (M_W, M_W), (1.0, 1.0),
                            tm=mems.shape[0])
        ya, ym = _attn_am(qa, ka, va, qm, mk, mv, a_sinks[l], rel_bias, buckets_a,
                          g_out[l, :A_Q], g_out[l, A_Q + B_W:], tq=512)
        obs, lses = [], []
        for (window, dil), buckets in zip(B_PATTERNS, buckets_b):
            o, lse = _attn_b(qb, kb, vb, rel_bias, buckets, dil, tq=512)
            obs.append(o)
            lses.append(lse)
        x1, h2 = _out_proj(xs, ya, ym, obs, lses, g_out[l, A_Q:A_Q + B_W], w_out[l].astype(BF16),
                           g_mix_post[l], g_ffn_pre[l], tm=512)
        xs = _ffn(h2, x1, w_up[l].astype(BF16), conv_w[l], conv_b[l], w_down[l].astype(BF16),
                  g_ffn_post[l], tm=1024, tf=512, chunk=256)
    return xs.reshape(bn, s_len, d)
```

```python
import functools
import math

import jax
import jax.numpy as jnp
from jax import lax
from jax.experimental import pallas as pl
from jax.experimental.pallas import tpu as pltpu

F32 = jnp.float32
BF16 = jnp.bfloat16
U32 = jnp.uint32

HEAD_DIM = 128
BLOCK = 128
A_HEADS, A_KV_HEADS, A_GROUP = 6, 2, 3
A_WINDOW = 128
B_HEADS = 6
B_PATTERNS = ((128, 1), (512, 4), (2048, 16))
M_HEADS = 4
REL_BUCKETS, REL_MAX_EXACT, REL_MAX_DISTANCE = 32, 16, 2048
CONV_WIDTH = 3
EPS = 1e-6
NEG = -1e30

A_Q = A_HEADS * HEAD_DIM
A_KV = A_KV_HEADS * HEAD_DIM
B_W = B_HEADS * HEAD_DIM
M_W = M_HEADS * HEAD_DIM
B_QW = B_W // 2
Q_SCALE = HEAD_DIM ** -0.5

VMEM_LIMIT_BYTES = 56 * 1024 * 1024


def _params(n_axes):
    return pltpu.CompilerParams(dimension_semantics=("arbitrary",) * n_axes,
                                vmem_limit_bytes=VMEM_LIMIT_BYTES)


def _rms(x, g):
    return x * lax.rsqrt(jnp.mean(x * x, axis=-1, keepdims=True) + EPS) * g


def _dot_nt(a, b):
    return lax.dot_general(a, b, (((1,), (1,)), ((), ())), preferred_element_type=F32)


def _softmax_parts(s, sink=None):
    m = jnp.max(s, axis=-1, keepdims=True)
    if sink is not None:
        m = jnp.maximum(m, sink)
    p = jnp.exp(s - m)
    den = jnp.sum(p, axis=-1, keepdims=True)
    if sink is not None:
        den = den + jnp.exp(sink - m)
    return p, m, den


def _pack_pair(hi, lo):
    hi_bits = lax.bitcast_convert_type(hi.astype(BF16).astype(F32), U32)
    lo_bits = lax.bitcast_convert_type(lo.astype(BF16).astype(F32), U32)
    return hi_bits | (lo_bits >> 16)


def _unpack_hi(words):
    return lax.bitcast_convert_type(words & jnp.uint32(0xFFFF0000), F32).astype(BF16)


def _unpack_lo(words):
    return lax.bitcast_convert_type(words << 16, F32).astype(BF16)


def _proj_kernel(x_ref, g_ref, w_ref, *o_refs, outs):
    h = _rms(x_ref[...], g_ref[...]).astype(BF16)

    def cols(off, width, scale):
        acc = jnp.dot(h, w_ref[:, off:off + width], preferred_element_type=F32)
        return acc if scale == 1.0 else acc * scale

    for o_ref, (offs, width, scale) in zip(o_refs, outs):
        if len(offs) == 1:
            o_ref[...] = cols(offs[0], width, scale).astype(BF16)
        else:
            o_ref[...] = _pack_pair(cols(offs[0], width, scale), cols(offs[1], width, scale))


def _norm_proj(x, g, w, outs, tm):
    rows, d = x.shape
    n = w.shape[1]
    assert rows % tm == 0
    return pl.pallas_call(
        functools.partial(_proj_kernel, outs=outs),
        grid=(rows // tm,),
        in_specs=[
            pl.BlockSpec((tm, d), lambda i: (i, 0)),
            pl.BlockSpec((1, d), lambda i: (0, 0)),
            pl.BlockSpec((d, n), lambda i: (0, 0), pipeline_mode=pl.Buffered(1)),
        ],
        out_specs=[pl.BlockSpec((tm, width), lambda i: (i, 0)) for _, width, _ in outs],
        out_shape=[jax.ShapeDtypeStruct((rows, width), BF16 if len(offs) == 1 else U32)
                   for offs, width, _ in outs],
        compiler_params=_params(1),
        name="norm_proj",
    )(x, g.reshape(1, d), w)


def _lookup_bias(table_ref, bucket, head):
    def body(b, acc):
        return jnp.where(bucket == b, table_ref[b, head], acc)
    return lax.fori_loop(0, REL_BUCKETS, body, jnp.full(bucket.shape, NEG, F32))


def _attn_am_kernel(sink_ref, table_ref, bucket_ref, qa_ref, ka_ref, kap_ref, va_ref, vap_ref,
                    qm_ref, mk_ref, mv_ref, ga_ref, gm_ref, ya_ref, ym_ref, bias_scr, *, nsub):
    i = pl.program_id(0)

    @pl.when(i == 0)
    def _():
        for var in range(2):
            for h in range(A_HEADS):
                kv, g = divmod(h, A_GROUP)
                bias_scr[var, kv, g * BLOCK:(g + 1) * BLOCK, :] = _lookup_bias(
                    table_ref, bucket_ref[var], h)

    for sb in range(nsub):
        rows = slice(sb * BLOCK, (sb + 1) * BLOCK)
        prev_rows = slice((sb - 1) * BLOCK, sb * BLOCK)
        b_ref = bias_scr.at[jnp.minimum(i, 1) if sb == 0 else 1]
        outs = []
        for kv in range(A_KV_HEADS):
            cols = slice(kv * HEAD_DIM, (kv + 1) * HEAD_DIM)
            if sb == 0:
                k_prev, v_prev = kap_ref[:, cols], vap_ref[:, cols]
            else:
                k_prev, v_prev = ka_ref[prev_rows, cols], va_ref[prev_rows, cols]
            kk = jnp.concatenate([k_prev, ka_ref[rows, cols]], axis=0)
            vv = jnp.concatenate([v_prev, va_ref[rows, cols]], axis=0)
            heads = [kv * A_GROUP + g for g in range(A_GROUP)]
            q3 = jnp.concatenate(
                [qa_ref[rows, h * HEAD_DIM:(h + 1) * HEAD_DIM] for h in heads], axis=0)
            s = _dot_nt(q3, kk) + b_ref[kv]
            sink = jnp.concatenate(
                [jnp.full((BLOCK, 1), sink_ref[h], F32) for h in heads], axis=0)
            p, _, den = _softmax_parts(s, sink)
            o = jnp.dot(p.astype(BF16), vv, preferred_element_type=F32) / den
            outs += [o[g * BLOCK:(g + 1) * BLOCK] for g in range(A_GROUP)]
        oa = jnp.concatenate(outs, axis=1)
        ya_ref[rows, :] = _rms(oa, ga_ref[...]).astype(BF16)

    outs = []
    for h in range(M_HEADS):
        cols = slice(h * HEAD_DIM, (h + 1) * HEAD_DIM)
        s = _dot_nt(qm_ref[:, cols], mk_ref[:, cols])
        p, _, den = _softmax_parts(s)
        outs.append(jnp.dot(p.astype(BF16), mv_ref[:, cols], preferred_element_type=F32) / den)
    ym_ref[...] = _rms(jnp.concatenate(outs, axis=1), gm_ref[...]).astype(BF16)


def _attn_am(qa, ka, va, qm, mk, mv, sinks, table, buckets, g_a, g_m, tq):
    s_len = qa.shape[0]
    nsub = tq // BLOCK
    mem_len = mk.shape[0]
    prev = lambda i: (jnp.maximum(i * nsub - 1, 0), 0)
    return pl.pallas_call(
        functools.partial(_attn_am_kernel, nsub=nsub),
        grid=(s_len // tq,),
        in_specs=[
            pl.BlockSpec(memory_space=pltpu.SMEM),
            pl.BlockSpec(memory_space=pltpu.SMEM),
            pl.BlockSpec(buckets.shape, lambda i: (0, 0, 0)),
            pl.BlockSpec((tq, A_Q), lambda i: (i, 0)),
            pl.BlockSpec((tq, A_KV), lambda i: (i, 0)),
            pl.BlockSpec((BLOCK, A_KV), prev),
            pl.BlockSpec((tq, A_KV), lambda i: (i, 0)),
            pl.BlockSpec((BLOCK, A_KV), prev),
            pl.BlockSpec((tq, M_W), lambda i: (i, 0)),
            pl.BlockSpec((mem_len, M_W), lambda i: (0, 0)),
            pl.BlockSpec((mem_len, M_W), lambda i: (0, 0)),
            pl.BlockSpec((1, A_Q), lambda i: (0, 0)),
            pl.BlockSpec((1, M_W), lambda i: (0, 0)),
        ],
        out_specs=[pl.BlockSpec((tq, A_Q), lambda i: (i, 0)),
                   pl.BlockSpec((tq, M_W), lambda i: (i, 0))],
        out_shape=[jax.ShapeDtypeStruct((s_len, A_Q), BF16),
                   jax.ShapeDtypeStruct((s_len, M_W), BF16)],
        scratch_shapes=[pltpu.VMEM((2, A_KV_HEADS, A_GROUP * BLOCK, 2 * BLOCK), F32)],
        compiler_params=_params(1),
        name="attn_am",
    )(sinks, table, buckets, qa, ka, ka, va, va, qm, mk, mv,
      g_a.reshape(1, A_Q), g_m.reshape(1, M_W))


def _build_bias_b(table_ref, bucket_ref, bias_scr):
    for var in range(2):
        for h in range(B_HEADS):
            bias_scr[var, h] = _lookup_bias(table_ref, bucket_ref[var], A_HEADS + h)


def _attn_b_tiles(qw_ref, kvw_ref, kvw_prev_ref, bias_scr, first_var, o_ref, lse_ref, nsub):
    ns = qw_ref.shape[0]
    per = BLOCK // ns
    half = B_HEADS // 2
    lane = lax.broadcasted_iota(jnp.int32, (BLOCK, HEAD_DIM), 1)

    def block(ref, sb, cols):
        return jnp.concatenate([ref[st, sb * per:(sb + 1) * per, cols] for st in range(ns)], axis=0)

    for sb in range(nsub):
        b_ref = bias_scr.at[first_var if sb == 0 else 1]
        lse_tile = jnp.zeros((BLOCK, HEAD_DIM), F32)
        for h in range(B_HEADS):
            cols = slice(h * HEAD_DIM, (h + 1) * HEAD_DIM)
            qw = block(qw_ref, sb, slice((h % half) * HEAD_DIM, (h % half + 1) * HEAD_DIM))
            q = _unpack_hi(qw) if h < half else _unpack_lo(qw)
            w_prev = block(kvw_prev_ref, 0, cols) if sb == 0 else block(kvw_ref, sb - 1, cols)
            w_cur = block(kvw_ref, sb, cols)
            kk = jnp.concatenate([_unpack_hi(w_prev), _unpack_hi(w_cur)], axis=0)
            vv = jnp.concatenate([_unpack_lo(w_prev), _unpack_lo(w_cur)], axis=0)
            s = _dot_nt(q, kk) + b_ref[h]
            p, m, den = _softmax_parts(s)
            o = jnp.dot(p.astype(BF16), vv, preferred_element_type=F32) / den
            for st in range(ns):
                o_ref[st, sb * per:(sb + 1) * per, cols] = o[st * per:(st + 1) * per]
            lse_tile = jnp.where(lane == h, m + jnp.log(den), lse_tile)
        for st in range(ns):
            lse_ref[st, sb * per:(sb + 1) * per, :] = lse_tile[st * per:(st + 1) * per]


def _attn_b_nat_kernel(table_ref, bucket_ref, qw_ref, kvw_ref, kvwp_ref, o_ref, lse_ref, bias_scr,
                       *, nsub):
    i = pl.program_id(0)

    @pl.when(i == 0)
    def _():
        _build_bias_b(table_ref, bucket_ref, bias_scr)

    _attn_b_tiles(qw_ref, kvw_ref, kvwp_ref, bias_scr, jnp.minimum(i, 1), o_ref, lse_ref, nsub)


def _attn_b_dil_kernel(table_ref, bucket_ref, qw_hbm, kvw_hbm, o_hbm, lse_hbm,
                       qbuf, kvbuf, obuf, lbuf, carry, bias_scr, in_sems, out_sems,
                       *, nsub, steps_per_class, dil):
    s, n_steps = pl.program_id(0), pl.num_programs(0)
    ns = qbuf.shape[1]
    rows_per = qbuf.shape[2]
    slot = lax.rem(s, 2)

    def class_rows(step):
        c, i = lax.div(step, steps_per_class), lax.rem(step, steps_per_class)
        return [(pl.ds(i * rows_per, rows_per), c + st * dil) for st in range(ns)]

    def in_copies(step, sl):
        return [pltpu.make_async_copy(hbm.at[rows, col, :], buf.at[sl, st], in_sems.at[a, st, sl])
                for a, (hbm, buf) in enumerate(((qw_hbm, qbuf), (kvw_hbm, kvbuf)))
                for st, (rows, col) in enumerate(class_rows(step))]

    def out_copies(step, sl):
        return [pltpu.make_async_copy(buf.at[sl, st], hbm.at[rows, col, :], out_sems.at[a, st, sl])
                for a, (hbm, buf) in enumerate(((o_hbm, obuf), (lse_hbm, lbuf)))
                for st, (rows, col) in enumerate(class_rows(step))]

    @pl.when(s == 0)
    def _():
        for cp in in_copies(0, 0):
            cp.start()
        _build_bias_b(table_ref, bucket_ref, bias_scr)
        carry[...] = jnp.zeros(carry.shape, U32)

    @pl.when(s + 1 < n_steps)
    def _():
        for cp in in_copies(s + 1, 1 - slot):
            cp.start()

    for cp in in_copies(s, slot):
        cp.wait()

    @pl.when(s >= 2)
    def _():
        for cp in out_copies(s - 2, slot):
            cp.wait()

    first_var = jnp.minimum(lax.rem(s, steps_per_class), 1)
    _attn_b_tiles(qbuf.at[slot], kvbuf.at[slot], carry, bias_scr, first_var,
                  obuf.at[slot], lbuf.at[slot], nsub)
    carry[...] = kvbuf[slot, :, rows_per - carry.shape[1]:, :]

    for cp in out_copies(s, slot):
        cp.start()

    @pl.when(s == n_steps - 1)
    def _():
        for cp in out_copies(s - 1, 1 - slot) + out_copies(s, slot):
            cp.wait()


def _attn_b_nat(qw, kvw, table, buckets, tq):
    s_len = qw.shape[0]
    nsub = tq // BLOCK
    cur = lambda i: (0, i, 0)
    prev = lambda i: (0, jnp.maximum(i * nsub - 1, 0), 0)
    o, lse = pl.pallas_call(
        functools.partial(_attn_b_nat_kernel, nsub=nsub),
        grid=(s_len // tq,),
        in_specs=[
            pl.BlockSpec(memory_space=pltpu.SMEM),
            pl.BlockSpec(buckets.shape, lambda i: (0, 0, 0)),
            pl.BlockSpec((1, tq, B_QW), cur),
            pl.BlockSpec((1, tq, B_W), cur),
            pl.BlockSpec((1, BLOCK, B_W), prev),
        ],
        out_specs=[pl.BlockSpec((1, tq, B_W), cur), pl.BlockSpec((1, tq, HEAD_DIM), cur)],
        out_shape=[jax.ShapeDtypeStruct((1, s_len, B_W), F32),
                   jax.ShapeDtypeStruct((1, s_len, HEAD_DIM), F32)],
        scratch_shapes=[pltpu.VMEM((2, B_HEADS, BLOCK, 2 * BLOCK), F32)],
        compiler_params=_params(1),
        name="attn_b_d1",
    )(table, buckets, qw[None], kvw[None], kvw[None])
    return o[0], lse[0]


def _streams(dil):
    return 1 if dil == 1 else max(1, 8 // dil)


def _attn_b_dil(qw, kvw, table, buckets, dil, tq):
    s_len = qw.shape[0]
    ns = _streams(dil)
    n = dil * ns
    rows_per = tq // ns
    steps_per_class = s_len // dil // tq
    view = lambda t: t.reshape(s_len // n, n, t.shape[-1])
    any_spec = pl.BlockSpec(memory_space=pl.ANY)
    o, lse = pl.pallas_call(
        functools.partial(_attn_b_dil_kernel, nsub=tq // BLOCK, steps_per_class=steps_per_class,
                          dil=dil),
        grid=(dil * steps_per_class,),
        in_specs=[pl.BlockSpec(memory_space=pltpu.SMEM),
                  pl.BlockSpec(buckets.shape, lambda s: (0, 0, 0)), any_spec, any_spec],
        out_specs=[any_spec, any_spec],
        out_shape=[jax.ShapeDtypeStruct((s_len // n, n, B_W), F32),
                   jax.ShapeDtypeStruct((s_len // n, n, HEAD_DIM), F32)],
        scratch_shapes=[
            pltpu.VMEM((2, ns, rows_per, B_QW), U32),
            pltpu.VMEM((2, ns, rows_per, B_W), U32),
            pltpu.VMEM((2, ns, rows_per, B_W), F32),
            pltpu.VMEM((2, ns, rows_per, HEAD_DIM), F32),
            pltpu.VMEM((ns, BLOCK // ns, B_W), U32),
            pltpu.VMEM((2, B_HEADS, BLOCK, 2 * BLOCK), F32),
            pltpu.SemaphoreType.DMA((2, ns, 2)),
            pltpu.SemaphoreType.DMA((2, ns, 2)),
        ],
        compiler_params=_params(1),
        name=f"attn_b_d{dil}",
    )(table, buckets, view(qw), view(kvw))
    return o.reshape(s_len, B_W), lse.reshape(s_len, HEAD_DIM)


def _out_kernel(x_ref, ya_ref, ym_ref, o1_ref, o2_ref, o3_ref, l1_ref, l2_ref, l3_ref,
                gb_ref, w_ref, gpost_ref, gpre_ref, x1_ref, h2_ref):
    o_refs = (o1_ref, o2_ref, o3_ref)
    obs = []
    for h in range(B_HEADS):
        cols = slice(h * HEAD_DIM, (h + 1) * HEAD_DIM)
        ls = [l_ref[:, h:h + 1] for l_ref in (l1_ref, l2_ref, l3_ref)]
        mx = jnp.maximum(jnp.maximum(ls[0], ls[1]), ls[2])
        es = [jnp.exp(l - mx) for l in ls]
        inv = 1.0 / (es[0] + es[1] + es[2])
        obs.append(sum((e * inv) * o_ref[:, cols] for e, o_ref in zip(es, o_refs)))
    yb = _rms(jnp.concatenate(obs, axis=1), gb_ref[...]).astype(BF16)
    y_in = jnp.concatenate([ya_ref[...], yb, ym_ref[...]], axis=1)
    y = jnp.dot(y_in, w_ref[...], preferred_element_type=F32)
    x1 = x_ref[...] + _rms(y, gpost_ref[...])
    x1_ref[...] = x1
    h2_ref[...] = _rms(x1, gpre_ref[...]).astype(BF16)


def _out_proj(x, ya, ym, obs, lses, g_b, w_out, g_post, g_pre, tm):
    s_len, d = x.shape
    row = lambda width: pl.BlockSpec((tm, width), lambda i: (i, 0))
    vec = lambda width: pl.BlockSpec((1, width), lambda i: (0, 0))
    return pl.pallas_call(
        _out_kernel,
        grid=(s_len // tm,),
        in_specs=[row(d), row(A_Q), row(M_W), row(B_W), row(B_W), row(B_W),
                  row(HEAD_DIM), row(HEAD_DIM), row(HEAD_DIM), vec(B_W),
                  pl.BlockSpec(w_out.shape, lambda i: (0, 0), pipeline_mode=pl.Buffered(1)),
                  vec(d), vec(d)],
        out_specs=[row(d), row(d)],
        out_shape=[jax.ShapeDtypeStruct((s_len, d), F32), jax.ShapeDtypeStruct((s_len, d), BF16)],
        compiler_params=_params(1),
        name="out_proj",
    )(x, ya, ym, *obs, *lses, g_b.reshape(1, B_W), w_out, g_post.reshape(1, d),
      g_pre.reshape(1, d))


def _shift_rows(u, carry, shift):
    rolled = pltpu.roll(u, shift, axis=0)
    head = jnp.where(lax.broadcasted_iota(jnp.int32, carry.shape, 0) < shift,
                     pltpu.roll(carry, shift, axis=0), rolled[:8])
    return jnp.concatenate([head, rolled[8:]], axis=0)


def _causal_conv(u, carry, cw, cb):
    return (cw[0:1] * _shift_rows(u, carry, 2) + cw[1:2] * _shift_rows(u, carry, 1)
            + cw[2:3] * u + cb)


def _ffn_kernel(h_ref, wg_ref, wv_ref, cwg_ref, cwv_ref, cbg_ref, cbv_ref, wd_ref, x_ref, g_ref,
                o_ref, carry_g, carry_v, *, chunk):
    i, j = pl.program_id(0), pl.program_id(1)

    @pl.when(i == 0)
    def _():
        carry_g[j] = jnp.zeros(carry_g.shape[1:], F32)
        carry_v[j] = jnp.zeros(carry_v.shape[1:], F32)

    @pl.when(j == 0)
    def _():
        o_ref[...] = jnp.zeros(o_ref.shape, F32)

    prev_g, prev_v = carry_g[j], carry_v[j]
    for c in range(h_ref.shape[0] // chunk):
        rows = slice(c * chunk, (c + 1) * chunk)
        h = h_ref[rows, :]
        ug = jnp.dot(h, wg_ref[...], preferred_element_type=F32)
        uv = jnp.dot(h, wv_ref[...], preferred_element_type=F32)
        cg = _causal_conv(ug, prev_g, cwg_ref[...], cbg_ref[...])
        cv = _causal_conv(uv, prev_v, cwv_ref[...], cbv_ref[...])
        prev_g, prev_v = ug[-8:], uv[-8:]
        act = (jax.nn.gelu(cg, approximate=True) * cv).astype(BF16)
        o_ref[rows, :] += jnp.dot(act, wd_ref[...], preferred_element_type=F32)
    carry_g[j] = prev_g
    carry_v[j] = prev_v

    @pl.when(j == pl.num_programs(1) - 1)
    def _():
        o_ref[...] = x_ref[...] + _rms(o_ref[...], g_ref[...])


def _ffn(h2, x1, w_up, conv_w, conv_b, w_down, g_post, tm, tf, chunk, x_buffers):
    s_len, d = x1.shape
    d_ff = w_down.shape[0]
    nf = d_ff // tf
    assert d_ff % tf == 0 and s_len % tm == 0 and tm % chunk == 0
    return pl.pallas_call(
        functools.partial(_ffn_kernel, chunk=chunk),
        grid=(s_len // tm, nf),
        in_specs=[
            pl.BlockSpec((tm, d), lambda i, j: (i, 0)),
            pl.BlockSpec((d, tf), lambda i, j: (0, j)),
            pl.BlockSpec((d, tf), lambda i, j: (0, nf + j)),
            pl.BlockSpec((CONV_WIDTH, tf), lambda i, j: (0, j)),
            pl.BlockSpec((CONV_WIDTH, tf), lambda i, j: (0, nf + j)),
            pl.BlockSpec((1, tf), lambda i, j: (0, j)),
            pl.BlockSpec((1, tf), lambda i, j: (0, nf + j)),
            pl.BlockSpec((tf, d), lambda i, j: (j, 0)),
            pl.BlockSpec((tm, d), lambda i, j: (i, 0), pipeline_mode=pl.Buffered(x_buffers)),
            pl.BlockSpec((1, d), lambda i, j: (0, 0)),
        ],
        out_specs=pl.BlockSpec((tm, d), lambda i, j: (i, 0)),
        out_shape=jax.ShapeDtypeStruct((s_len, d), F32),
        scratch_shapes=[pltpu.VMEM((nf, 8, tf), F32), pltpu.VMEM((nf, 8, tf), F32)],
        compiler_params=_params(2),
        name="conv_glu_ffn",
    )(h2, w_up, w_up, conv_w, conv_w, conv_b.reshape(1, -1), conv_b.reshape(1, -1), w_down,
      x1, g_post.reshape(1, d))


def _rel_bucket(dist):
    n = jnp.maximum(dist, 0)
    nf = jnp.maximum(n, 1).astype(F32)
    large = REL_MAX_EXACT + (jnp.log(nf / REL_MAX_EXACT)
                             / math.log(REL_MAX_DISTANCE / REL_MAX_EXACT)
                             * (REL_BUCKETS - REL_MAX_EXACT)).astype(jnp.int32)
    large = jnp.minimum(large, REL_BUCKETS - 1)
    return jnp.where(n < REL_MAX_EXACT, n, large)


def _bucket_blocks(dilation, max_dist, streams=1):
    per = BLOCK // streams
    idx = jnp.arange(BLOCK)
    member = streams * (idx % per) + idx // per
    key = jnp.concatenate([member, BLOCK + member])
    rel = member[:, None] + BLOCK - key[None, :]
    valid = (rel >= 0) & (rel <= max_dist)
    normal = jnp.where(valid, _rel_bucket(rel * dilation), -1)
    first = jnp.where((jnp.arange(2 * BLOCK) >= BLOCK)[None, :], normal, -1)
    return jnp.stack([first, normal]).astype(jnp.int32)


def kernel(x, mem, rel_bias, g_mix_pre, w_in, a_sinks, g_mem, w_mem_kv, g_out, w_out, g_mix_post,
           g_ffn_pre, w_up, conv_w, conv_b, w_down, g_ffn_post):
    bn, s_len, d = x.shape
    assert bn == 1
    depth = w_in.shape[0]
    xs = x.reshape(s_len, d)
    mems = mem.reshape(mem.shape[1], d)

    buckets_a = _bucket_blocks(1, A_WINDOW - 1)
    buckets_b = [_bucket_blocks(dil, window // dil, _streams(dil)) for window, dil in B_PATTERNS]

    off_ka, off_va, off_qb = A_Q, A_Q + A_KV, A_Q + 2 * A_KV
    off_kb, off_vb, off_qm = off_qb + B_W, off_qb + 2 * B_W, off_qb + 3 * B_W
    proj_outs = (((0,), A_Q, Q_SCALE), ((off_ka,), A_KV, 1.0), ((off_va,), A_KV, 1.0),
                 ((off_qb, off_qb + B_QW), B_QW, Q_SCALE), ((off_kb, off_vb), B_W, 1.0),
                 ((off_qm,), M_W, Q_SCALE))
    mem_outs = (((0,), M_W, 1.0), ((M_W,), M_W, 1.0))

    for l in range(depth):
        qa, ka, va, qbw, kvbw, qm = _norm_proj(
            xs, g_mix_pre[l], w_in[l].astype(BF16), proj_outs, tm=512)
        mk, mv = _norm_proj(mems, g_mem[l], w_mem_kv[l].astype(BF16), mem_outs,
                            tm=mems.shape[0])
        ya, ym = _attn_am(qa, ka, va, qm, mk, mv, a_sinks[l], rel_bias, buckets_a,
                          g_out[l, :A_Q], g_out[l, A_Q + B_W:], tq=512)
        obs, lses = [], []
        for (window, dil), buckets in zip(B_PATTERNS, buckets_b):
            if dil == 1:
                o, lse = _attn_b_nat(qbw, kvbw, rel_bias, buckets, tq=512)
            else:
                o, lse = _attn_b_dil(qbw, kvbw, rel_bias, buckets, dil, tq=512)
            obs.append(o)
            lses.append(lse)
        x1, h2 = _out_proj(xs, ya, ym, obs, lses, g_out[l, A_Q:A_Q + B_W], w_out[l].astype(BF16),
                           g_mix_post[l], g_ffn_pre[l], tm=512)
        xs = _ffn(h2, x1, w_up[l].astype(BF16), conv_w[l], conv_b[l], w_down[l].astype(BF16),
                  g_ffn_post[l], tf=512, **((dict(tm=512, chunk=256, x_buffers=2),
                                             dict(tm=1024, chunk=512, x_buffers=1))[l % 2]))
    return xs.reshape(bn, s_len, d)
```

```python
import functools
import math

import jax
import jax.numpy as jnp
from jax import lax
from jax.experimental import pallas as pl
from jax.experimental.pallas import tpu as pltpu

F32 = jnp.float32
BF16 = jnp.bfloat16
U32 = jnp.uint32

HEAD_DIM = 128
BLOCK = 128
A_HEADS, A_KV_HEADS, A_GROUP = 6, 2, 3
A_WINDOW = 128
B_HEADS = 6
B_PATTERNS = ((128, 1), (512, 4), (2048, 16))
M_HEADS = 4
REL_BUCKETS, REL_MAX_EXACT, REL_MAX_DISTANCE = 32, 16, 2048
CONV_WIDTH = 3
EPS = 1e-6
NEG = -1e30

A_Q = A_HEADS * HEAD_DIM
A_KV = A_KV_HEADS * HEAD_DIM
B_W = B_HEADS * HEAD_DIM
M_W = M_HEADS * HEAD_DIM
B_QW = B_W // 2
LOG2E = math.log2(math.e)
Q_SCALE = HEAD_DIM ** -0.5 * LOG2E

VMEM_LIMIT_BYTES = 56 * 1024 * 1024
MXU_COLS = 256


def _params(n_axes):
    return pltpu.CompilerParams(dimension_semantics=("arbitrary",) * n_axes,
                                vmem_limit_bytes=VMEM_LIMIT_BYTES)


def _rms(x, g):
    return x * lax.rsqrt(jnp.mean(x * x, axis=-1, keepdims=True) + EPS) * g


def _dot_nt(a, b):
    return lax.dot_general(a, b, (((1,), (1,)), ((), ())), preferred_element_type=F32)


def _softmax_parts(s, sink=None):
    m = jnp.max(s, axis=-1, keepdims=True)
    if sink is not None:
        m = jnp.maximum(m, sink)
    p = jnp.exp2(s - m)
    den = jnp.sum(p, axis=-1, keepdims=True)
    if sink is not None:
        den = den + jnp.exp2(sink - m)
    return p, m, den


def _pack_pair(hi, lo):
    hi_bits = lax.bitcast_convert_type(hi.astype(BF16).astype(F32), U32)
    lo_bits = lax.bitcast_convert_type(lo.astype(BF16).astype(F32), U32)
    return hi_bits | (lo_bits >> 16)


def _unpack_hi(words):
    return lax.bitcast_convert_type(words & jnp.uint32(0xFFFF0000), F32).astype(BF16)


def _unpack_lo(words):
    return lax.bitcast_convert_type(words << 16, F32).astype(BF16)


def _proj_kernel(x_ref, g_ref, w_ref, *o_refs, outs):
    h = _rms(x_ref[...], g_ref[...]).astype(BF16)

    def cols(off, width, scale):
        acc = jnp.dot(h, w_ref[:, off:off + width], preferred_element_type=F32)
        return acc if scale == 1.0 else acc * scale

    for o_ref, (offs, width, scale) in zip(o_refs, outs):
        if len(offs) == 1:
            o_ref[...] = cols(offs[0], width, scale).astype(BF16)
        else:
            o_ref[...] = _pack_pair(cols(offs[0], width, scale), cols(offs[1], width, scale))


def _norm_proj(x, g, w, outs, tm):
    rows, d = x.shape
    n = w.shape[1]
    assert rows % tm == 0
    return pl.pallas_call(
        functools.partial(_proj_kernel, outs=outs),
        grid=(rows // tm,),
        in_specs=[
            pl.BlockSpec((tm, d), lambda i: (i, 0)),
            pl.BlockSpec((1, d), lambda i: (0, 0)),
            pl.BlockSpec((d, n), lambda i: (0, 0), pipeline_mode=pl.Buffered(1)),
        ],
        out_specs=[pl.BlockSpec((tm, width), lambda i: (i, 0)) for _, width, _ in outs],
        out_shape=[jax.ShapeDtypeStruct((rows, width), BF16 if len(offs) == 1 else U32)
                   for offs, width, _ in outs],
        compiler_params=_params(1),
        name="norm_proj",
    )(x, g.reshape(1, d), w)


def _lookup_bias(table_ref, bucket, head):
    def body(b, acc):
        return jnp.where(bucket == b, table_ref[b, head] * LOG2E, acc)
    return lax.fori_loop(0, REL_BUCKETS, body, jnp.full(bucket.shape, NEG, F32))


def _attn_am_kernel(sink_ref, table_ref, bucket_ref, qa_ref, ka_ref, kap_ref, va_ref, vap_ref,
                    qm_ref, mk_ref, mv_ref, ga_ref, gm_ref, ya_ref, ym_ref, bias_scr, *, nsub):
    i = pl.program_id(0)

    @pl.when(i == 0)
    def _():
        for var in range(2):
            for h in range(A_HEADS):
                kv, g = divmod(h, A_GROUP)
                bias_scr[var, kv, g * BLOCK:(g + 1) * BLOCK, :] = _lookup_bias(
                    table_ref, bucket_ref[var], h)

    for sb in range(nsub):
        rows = slice(sb * BLOCK, (sb + 1) * BLOCK)
        prev_rows = slice((sb - 1) * BLOCK, sb * BLOCK)
        b_ref = bias_scr.at[jnp.minimum(i, 1) if sb == 0 else 1]
        outs = []
        for kv in range(A_KV_HEADS):
            cols = slice(kv * HEAD_DIM, (kv + 1) * HEAD_DIM)
            if sb == 0:
                k_prev, v_prev = kap_ref[:, cols], vap_ref[:, cols]
            else:
                k_prev, v_prev = ka_ref[prev_rows, cols], va_ref[prev_rows, cols]
            kk = jnp.concatenate([k_prev, ka_ref[rows, cols]], axis=0)
            vv = jnp.concatenate([v_prev, va_ref[rows, cols]], axis=0)
            heads = [kv * A_GROUP + g for g in range(A_GROUP)]
            q3 = jnp.concatenate(
                [qa_ref[rows, h * HEAD_DIM:(h + 1) * HEAD_DIM] for h in heads], axis=0)
            s = _dot_nt(q3, kk) + b_ref[kv]
            sink = jnp.concatenate(
                [jnp.full((BLOCK, 1), sink_ref[h] * LOG2E, F32) for h in heads], axis=0)
            p, _, den = _softmax_parts(s, sink)
            o = jnp.dot(p.astype(BF16), vv, preferred_element_type=F32) / den
            outs += [o[g * BLOCK:(g + 1) * BLOCK] for g in range(A_GROUP)]
        oa = jnp.concatenate(outs, axis=1)
        ya_ref[rows, :] = _rms(oa, ga_ref[...]).astype(BF16)

    outs = []
    for h in range(M_HEADS):
        cols = slice(h * HEAD_DIM, (h + 1) * HEAD_DIM)
        s = _dot_nt(qm_ref[:, cols], mk_ref[:, cols])
        p, _, den = _softmax_parts(s)
        outs.append(jnp.dot(p.astype(BF16), mv_ref[:, cols], preferred_element_type=F32) / den)
    ym_ref[...] = _rms(jnp.concatenate(outs, axis=1), gm_ref[...]).astype(BF16)


def _attn_am(qa, ka, va, qm, mk, mv, sinks, table, buckets, g_a, g_m, tq):
    s_len = qa.shape[0]
    nsub = tq // BLOCK
    mem_len = mk.shape[0]
    prev = lambda i: (jnp.maximum(i * nsub - 1, 0), 0)
    return pl.pallas_call(
        functools.partial(_attn_am_kernel, nsub=nsub),
        grid=(s_len // tq,),
        in_specs=[
            pl.BlockSpec(memory_space=pltpu.SMEM),
            pl.BlockSpec(memory_space=pltpu.SMEM),
            pl.BlockSpec(buckets.shape, lambda i: (0, 0, 0)),
            pl.BlockSpec((tq, A_Q), lambda i: (i, 0)),
            pl.BlockSpec((tq, A_KV), lambda i: (i, 0)),
            pl.BlockSpec((BLOCK, A_KV), prev),
            pl.BlockSpec((tq, A_KV), lambda i: (i, 0)),
            pl.BlockSpec((BLOCK, A_KV), prev),
            pl.BlockSpec((tq, M_W), lambda i: (i, 0)),
            pl.BlockSpec((mem_len, M_W), lambda i: (0, 0)),
            pl.BlockSpec((mem_len, M_W), lambda i: (0, 0)),
            pl.BlockSpec((1, A_Q), lambda i: (0, 0)),
            pl.BlockSpec((1, M_W), lambda i: (0, 0)),
        ],
        out_specs=[pl.BlockSpec((tq, A_Q), lambda i: (i, 0)),
                   pl.BlockSpec((tq, M_W), lambda i: (i, 0))],
        out_shape=[jax.ShapeDtypeStruct((s_len, A_Q), BF16),
                   jax.ShapeDtypeStruct((s_len, M_W), BF16)],
        scratch_shapes=[pltpu.VMEM((2, A_KV_HEADS, A_GROUP * BLOCK, 2 * BLOCK), F32)],
        compiler_params=_params(1),
        name="attn_am",
    )(sinks, table, buckets, qa, ka, ka, va, va, qm, mk, mv,
      g_a.reshape(1, A_Q), g_m.reshape(1, M_W))


def _build_bias_b(table_ref, bucket_ref, bias_scr):
    for var in range(2):
        for h in range(B_HEADS):
            bias_scr[var, h] = _lookup_bias(table_ref, bucket_ref[var], A_HEADS + h)


def _attn_b_tiles(qw_ref, kvw_ref, kvw_prev_ref, bias_scr, first_var, o_ref, lse_ref, nsub):
    ns = qw_ref.shape[0]
    per = BLOCK // ns
    half = B_HEADS // 2
    lane = lax.broadcasted_iota(jnp.int32, (BLOCK, HEAD_DIM), 1)

    def block(ref, sb, cols):
        return jnp.concatenate([ref[st, sb * per:(sb + 1) * per, cols] for st in range(ns)], axis=0)

    for sb in range(nsub):
        b_ref = bias_scr.at[first_var if sb == 0 else 1]
        lse_tile = jnp.zeros((BLOCK, HEAD_DIM), F32)
        for h in range(B_HEADS):
            cols = slice(h * HEAD_DIM, (h + 1) * HEAD_DIM)
            qw = block(qw_ref, sb, slice((h % half) * HEAD_DIM, (h % half + 1) * HEAD_DIM))
            q = _unpack_hi(qw) if h < half else _unpack_lo(qw)
            w_prev = block(kvw_prev_ref, 0, cols) if sb == 0 else block(kvw_ref, sb - 1, cols)
            w_cur = block(kvw_ref, sb, cols)
            kk = jnp.concatenate([_unpack_hi(w_prev), _unpack_hi(w_cur)], axis=0)
            vv = jnp.concatenate([_unpack_lo(w_prev), _unpack_lo(w_cur)], axis=0)
            s = _dot_nt(q, kk) + b_ref[h]
            p, m, den = _softmax_parts(s)
            o = jnp.dot(p.astype(BF16), vv, preferred_element_type=F32) / den
            for st in range(ns):
                o_ref[st, sb * per:(sb + 1) * per, cols] = o[st * per:(st + 1) * per]
            lse_tile = jnp.where(lane == h, m + jnp.log2(den), lse_tile)
        for st in range(ns):
            lse_ref[st, sb * per:(sb + 1) * per, :] = lse_tile[st * per:(st + 1) * per]


def _attn_b_nat_kernel(table_ref, bucket_ref, qw_ref, kvw_ref, kvwp_ref, o_ref, lse_ref, bias_scr,
                       *, nsub):
    i = pl.program_id(0)

    @pl.when(i == 0)
    def _():
        _build_bias_b(table_ref, bucket_ref, bias_scr)

    _attn_b_tiles(qw_ref, kvw_ref, kvwp_ref, bias_scr, jnp.minimum(i, 1), o_ref, lse_ref, nsub)


def _attn_b_dil_kernel(table_ref, bucket_ref, qw_hbm, kvw_hbm, o_hbm, lse_hbm,
                       qbuf, kvbuf, obuf, lbuf, carry, bias_scr, in_sems, out_sems,
                       *, nsub, steps_per_class, dil):
    s, n_steps = pl.program_id(0), pl.num_programs(0)
    ns = qbuf.shape[1]
    rows_per = qbuf.shape[2]
    slot = lax.rem(s, 2)

    def class_rows(step):
        c, i = lax.div(step, steps_per_class), lax.rem(step, steps_per_class)
        return [(pl.ds(i * rows_per, rows_per), c + st * dil) for st in range(ns)]

    def in_copies(step, sl):
        return [pltpu.make_async_copy(hbm.at[rows, col, :], buf.at[sl, st], in_sems.at[a, st, sl])
                for a, (hbm, buf) in enumerate(((qw_hbm, qbuf), (kvw_hbm, kvbuf)))
                for st, (rows, col) in enumerate(class_rows(step))]

    def out_copies(step, sl):
        return [pltpu.make_async_copy(buf.at[sl, st], hbm.at[rows, col, :], out_sems.at[a, st, sl])
                for a, (hbm, buf) in enumerate(((o_hbm, obuf), (lse_hbm, lbuf)))
                for st, (rows, col) in enumerate(class_rows(step))]

    @pl.when(s == 0)
    def _():
        for cp in in_copies(0, 0):
            cp.start()
        _build_bias_b(table_ref, bucket_ref, bias_scr)
        carry[...] = jnp.zeros(carry.shape, U32)

    @pl.when(s + 1 < n_steps)
    def _():
        for cp in in_copies(s + 1, 1 - slot):
            cp.start()

    for cp in in_copies(s, slot):
        cp.wait()

    @pl.when(s >= 2)
    def _():
        for cp in out_copies(s - 2, slot):
            cp.wait()

    first_var = jnp.minimum(lax.rem(s, steps_per_class), 1)
    _attn_b_tiles(qbuf.at[slot], kvbuf.at[slot], carry, bias_scr, first_var,
                  obuf.at[slot], lbuf.at[slot], nsub)
    carry[...] = kvbuf[slot, :, rows_per - carry.shape[1]:, :]

    for cp in out_copies(s, slot):
        cp.start()

    @pl.when(s == n_steps - 1)
    def _():
        for cp in out_copies(s - 1, 1 - slot) + out_copies(s, slot):
            cp.wait()


def _attn_b_nat(qw, kvw, table, buckets, tq):
    s_len = qw.shape[0]
    nsub = tq // BLOCK
    cur = lambda i: (0, i, 0)
    prev = lambda i: (0, jnp.maximum(i * nsub - 1, 0), 0)
    o, lse = pl.pallas_call(
        functools.partial(_attn_b_nat_kernel, nsub=nsub),
        grid=(s_len // tq,),
        in_specs=[
            pl.BlockSpec(memory_space=pltpu.SMEM),
            pl.BlockSpec(buckets.shape, lambda i: (0, 0, 0)),
            pl.BlockSpec((1, tq, B_QW), cur),
            pl.BlockSpec((1, tq, B_W), cur),
            pl.BlockSpec((1, BLOCK, B_W), prev),
        ],
        out_specs=[pl.BlockSpec((1, tq, B_W), cur), pl.BlockSpec((1, tq, HEAD_DIM), cur)],
        out_shape=[jax.ShapeDtypeStruct((1, s_len, B_W), F32),
                   jax.ShapeDtypeStruct((1, s_len, HEAD_DIM), F32)],
        scratch_shapes=[pltpu.VMEM((2, B_HEADS, BLOCK, 2 * BLOCK), F32)],
        compiler_params=_params(1),
        name="attn_b_d1",
    )(table, buckets, qw[None], kvw[None], kvw[None])
    return o[0], lse[0]


def _streams(dil):
    return 1 if dil == 1 else max(1, 8 // dil)


def _attn_b_dil(qw, kvw, table, buckets, dil, tq):
    s_len = qw.shape[0]
    ns = _streams(dil)
    n = dil * ns
    rows_per = tq // ns
    steps_per_class = s_len // dil // tq
    view = lambda t: t.reshape(s_len // n, n, t.shape[-1])
    any_spec = pl.BlockSpec(memory_space=pl.ANY)
    o, lse = pl.pallas_call(
        functools.partial(_attn_b_dil_kernel, nsub=tq // BLOCK, steps_per_class=steps_per_class,
                          dil=dil),
        grid=(dil * steps_per_class,),
        in_specs=[pl.BlockSpec(memory_space=pltpu.SMEM),
                  pl.BlockSpec(buckets.shape, lambda s: (0, 0, 0)), any_spec, any_spec],
        out_specs=[any_spec, any_spec],
        out_shape=[jax.ShapeDtypeStruct((s_len // n, n, B_W), F32),
                   jax.ShapeDtypeStruct((s_len // n, n, HEAD_DIM), F32)],
        scratch_shapes=[
            pltpu.VMEM((2, ns, rows_per, B_QW), U32),
            pltpu.VMEM((2, ns, rows_per, B_W), U32),
            pltpu.VMEM((2, ns, rows_per, B_W), F32),
            pltpu.VMEM((2, ns, rows_per, HEAD_DIM), F32),
            pltpu.VMEM((ns, BLOCK // ns, B_W), U32),
            pltpu.VMEM((2, B_HEADS, BLOCK, 2 * BLOCK), F32),
            pltpu.SemaphoreType.DMA((2, ns, 2)),
            pltpu.SemaphoreType.DMA((2, ns, 2)),
        ],
        compiler_params=_params(1),
        name=f"attn_b_d{dil}",
    )(table, buckets, view(qw), view(kvw))
    return o.reshape(s_len, B_W), lse.reshape(s_len, HEAD_DIM)


def _out_kernel(x_ref, ya_ref, ym_ref, o1_ref, o2_ref, o3_ref, l1_ref, l2_ref, l3_ref,
                gb_ref, w_ref, gpost_ref, gpre_ref, x1_ref, h2_ref, *, chunk):
    o_refs = (o1_ref, o2_ref, o3_ref)
    for c in range(x_ref.shape[0] // chunk):
        rows = slice(c * chunk, (c + 1) * chunk)
        obs = []
        for h in range(B_HEADS):
            cols = slice(h * HEAD_DIM, (h + 1) * HEAD_DIM)
            ls = [l_ref[rows, h:h + 1] for l_ref in (l1_ref, l2_ref, l3_ref)]
            mx = jnp.maximum(jnp.maximum(ls[0], ls[1]), ls[2])
            es = [jnp.exp2(l - mx) for l in ls]
            inv = 1.0 / (es[0] + es[1] + es[2])
            obs.append(sum((e * inv) * o_ref[rows, cols] for e, o_ref in zip(es, o_refs)))
        yb = _rms(jnp.concatenate(obs, axis=1), gb_ref[...]).astype(BF16)
        y_in = jnp.concatenate([ya_ref[rows, :], yb, ym_ref[rows, :]], axis=1)
        y = jnp.dot(y_in, w_ref[...], preferred_element_type=F32)
        x1 = x_ref[rows, :] + _rms(y, gpost_ref[...])
        x1_ref[rows, :] = x1
        h2_ref[rows, :] = _rms(x1, gpre_ref[...]).astype(BF16)


def _out_proj(x, ya, ym, obs, lses, g_b, w_out, g_post, g_pre, tm, chunk):
    s_len, d = x.shape
    row = lambda width: pl.BlockSpec((tm, width), lambda i: (i, 0))
    vec = lambda width: pl.BlockSpec((1, width), lambda i: (0, 0))
    return pl.pallas_call(
        functools.partial(_out_kernel, chunk=chunk),
        grid=(s_len // tm,),
        in_specs=[row(d), row(A_Q), row(M_W), row(B_W), row(B_W), row(B_W),
                  row(HEAD_DIM), row(HEAD_DIM), row(HEAD_DIM), vec(B_W),
                  pl.BlockSpec(w_out.shape, lambda i: (0, 0), pipeline_mode=pl.Buffered(1)),
                  vec(d), vec(d)],
        out_specs=[row(d), row(d)],
        out_shape=[jax.ShapeDtypeStruct((s_len, d), F32), jax.ShapeDtypeStruct((s_len, d), BF16)],
        compiler_params=_params(1),
        name="out_proj",
    )(x, ya, ym, *obs, *lses, g_b.reshape(1, B_W), w_out, g_post.reshape(1, d),
      g_pre.reshape(1, d))


def _shift_rows(u, carry, shift):
    rolled = pltpu.roll(u, shift, axis=0)
    head = jnp.where(lax.broadcasted_iota(jnp.int32, carry.shape, 0) < shift,
                     pltpu.roll(carry, shift, axis=0), rolled[:8])
    return jnp.concatenate([head, rolled[8:]], axis=0)


def _causal_conv(u, carry, cw, cb):
    return (cw[0:1] * _shift_rows(u, carry, 2) + cw[1:2] * _shift_rows(u, carry, 1)
            + cw[2:3] * u + cb)


def _ffn_kernel(h_ref, wg_ref, wv_ref, cwg_ref, cwv_ref, cbg_ref, cbv_ref, wd_ref, x_ref, g_ref,
                o_ref, ug_a, uv_a, ug_b, uv_b, carry_g, carry_v, *, nf):
    s = pl.program_id(0)
    prev = jnp.maximum(s - 1, 0)
    i, j = lax.div(prev, nf), lax.rem(prev, nf)

    @pl.when(s == 0)
    def _():
        ug_b[...] = jnp.zeros(ug_b.shape, F32)
        uv_b[...] = jnp.zeros(uv_b.shape, F32)

    @pl.when(i == 0)
    def _():
        carry_g[j] = jnp.zeros(carry_g.shape[1:], F32)
        carry_v[j] = jnp.zeros(carry_v.shape[1:], F32)

    @pl.when(j == 0)
    def _():
        o_ref[...] = jnp.zeros(o_ref.shape, F32)

    def stages(ug_up, uv_up, ug_dn, uv_dn):
        prev_g, prev_v = carry_g[j], carry_v[j]
        h = h_ref[...]
        tf = ug_dn.shape[1]
        n_piece = 2 * tf // MXU_COLS
        lanes = tf // n_piece
        acts = []
        for p in range(n_piece):
            cols = slice(p * lanes, (p + 1) * lanes)
            cg = _causal_conv(ug_dn[:, cols], prev_g[:, cols], cwg_ref[:, cols], cbg_ref[:, cols])
            cv = _causal_conv(uv_dn[:, cols], prev_v[:, cols], cwv_ref[:, cols], cbv_ref[:, cols])
            acts.append((jax.nn.gelu(cg, approximate=True) * cv).astype(BF16))
            w_ref, up_ref = (wg_ref, ug_up) if p < n_piece // 2 else (wv_ref, uv_up)
            tile = slice((p % (n_piece // 2)) * MXU_COLS, (p % (n_piece // 2) + 1) * MXU_COLS)
            up_ref[:, tile] = jnp.dot(h, w_ref[:, tile], preferred_element_type=F32)
        act = jnp.concatenate(acts, axis=1)
        o_ref[...] += jnp.dot(act, wd_ref[...], preferred_element_type=F32)
        carry_g[j] = ug_dn[-8:, :]
        carry_v[j] = uv_dn[-8:, :]

    even = lax.rem(s, 2) == 0
    pl.when(even)(lambda: stages(ug_a, uv_a, ug_b, uv_b))
    pl.when(jnp.logical_not(even))(lambda: stages(ug_b, uv_b, ug_a, uv_a))

    @pl.when((j == nf - 1) & (s > 0))
    def _():
        o_ref[...] = x_ref[...] + _rms(o_ref[...], g_ref[...])


def _ffn(h2, x1, w_up, conv_w, conv_b, w_down, g_post, tm, tf):
    s_len, d = x1.shape
    d_ff = w_down.shape[0]
    nf = d_ff // tf
    n_tiles = (s_len // tm) * nf
    assert d_ff % tf == 0 and s_len % tm == 0
    up = lambda s: jnp.minimum(s, n_tiles - 1)
    down = lambda s: jnp.maximum(s - 1, 0)
    single = pl.Buffered(1)
    return pl.pallas_call(
        functools.partial(_ffn_kernel, nf=nf),
        grid=(n_tiles + 1,),
        in_specs=[
            pl.BlockSpec((tm, d), lambda s: (up(s) // nf, 0), pipeline_mode=single),
            pl.BlockSpec((d, tf), lambda s: (0, up(s) % nf)),
            pl.BlockSpec((d, tf), lambda s: (0, nf + up(s) % nf)),
            pl.BlockSpec((CONV_WIDTH, tf), lambda s: (0, down(s) % nf)),
            pl.BlockSpec((CONV_WIDTH, tf), lambda s: (0, nf + down(s) % nf)),
            pl.BlockSpec((1, tf), lambda s: (0, down(s) % nf)),
            pl.BlockSpec((1, tf), lambda s: (0, nf + down(s) % nf)),
            pl.BlockSpec((tf, d), lambda s: (down(s) % nf, 0)),
            pl.BlockSpec((tm, d), lambda s: (down(s) // nf, 0), pipeline_mode=single),
            pl.BlockSpec((1, d), lambda s: (0, 0)),
        ],
        out_specs=pl.BlockSpec((tm, d), lambda s: (down(s) // nf, 0)),
        out_shape=jax.ShapeDtypeStruct((s_len, d), F32),
        scratch_shapes=[pltpu.VMEM((tm, tf), F32)] * 4
        + [pltpu.VMEM((nf, 8, tf), F32), pltpu.VMEM((nf, 8, tf), F32)],
        compiler_params=_params(1),
        name="conv_glu_ffn",
    )(h2, w_up, w_up, conv_w, conv_w, conv_b.reshape(1, -1), conv_b.reshape(1, -1), w_down,
      x1, g_post.reshape(1, d))


def _rel_bucket(dist):
    n = jnp.maximum(dist, 0)
    nf = jnp.maximum(n, 1).astype(F32)
    large = REL_MAX_EXACT + (jnp.log(nf / REL_MAX_EXACT)
                             / math.log(REL_MAX_DISTANCE / REL_MAX_EXACT)
                             * (REL_BUCKETS - REL_MAX_EXACT)).astype(jnp.int32)
    large = jnp.minimum(large, REL_BUCKETS - 1)
    return jnp.where(n < REL_MAX_EXACT, n, large)


def _bucket_blocks(dilation, max_dist, streams=1):
    per = BLOCK // streams
    idx = jnp.arange(BLOCK)
    member = streams * (idx % per) + idx // per
    key = jnp.concatenate([member, BLOCK + member])
    rel = member[:, None] + BLOCK - key[None, :]
    valid = (rel >= 0) & (rel <= max_dist)
    normal = jnp.where(valid, _rel_bucket(rel * dilation), -1)
    first = jnp.where((jnp.arange(2 * BLOCK) >= BLOCK)[None, :], normal, -1)
    return jnp.stack([first, normal]).astype(jnp.int32)


def kernel(x, mem, rel_bias, g_mix_pre, w_in, a_sinks, g_mem, w_mem_kv, g_out, w_out, g_mix_post,
           g_ffn_pre, w_up, conv_w, conv_b, w_down, g_ffn_post):
    bn, s_len, d = x.shape
    assert bn == 1
    depth = w_in.shape[0]
    xs = x.reshape(s_len, d)
    mems = mem.reshape(mem.shape[1], d)

    buckets_a = _bucket_blocks(1, A_WINDOW - 1)
    buckets_b = [_bucket_blocks(dil, window // dil, _streams(dil)) for window, dil in B_PATTERNS]

    off_ka, off_va, off_qb = A_Q, A_Q + A_KV, A_Q + 2 * A_KV
    off_kb, off_vb, off_qm = off_qb + B_W, off_qb + 2 * B_W, off_qb + 3 * B_W
    proj_outs = (((0,), A_Q, Q_SCALE), ((off_ka,), A_KV, 1.0), ((off_va,), A_KV, 1.0),
                 ((off_qb, off_qb + B_QW), B_QW, Q_SCALE), ((off_kb, off_vb), B_W, 1.0),
                 ((off_qm,), M_W, Q_SCALE))
    mem_outs = (((0,), M_W, 1.0), ((M_W,), M_W, 1.0))

    for l in range(depth):
        qa, ka, va, qbw, kvbw, qm = _norm_proj(
            xs, g_mix_pre[l], w_in[l].astype(BF16), proj_outs, tm=512)
        mk, mv = _norm_proj(mems, g_mem[l], w_mem_kv[l].astype(BF16), mem_outs,
                            tm=mems.shape[0])
        ya, ym = _attn_am(qa, ka, va, qm, mk, mv, a_sinks[l], rel_bias, buckets_a,
                          g_out[l, :A_Q], g_out[l, A_Q + B_W:], tq=(512, 1024)[l % 2])
        obs, lses = [], []
        for (window, dil), buckets in zip(B_PATTERNS, buckets_b):
            if dil == 1:
                o, lse = _attn_b_nat(qbw, kvbw, rel_bias, buckets, tq=(512, 1024)[l % 2])
            else:
                o, lse = _attn_b_dil(qbw, kvbw, rel_bias, buckets, dil, tq=(512, 1024)[l % 2])
            obs.append(o)
            lses.append(lse)
        x1, h2 = _out_proj(xs, ya, ym, obs, lses, g_out[l, A_Q:A_Q + B_W], w_out[l].astype(BF16),
                           g_mix_post[l], g_ffn_pre[l], tm=512, chunk=(256, 128)[l % 2])
        xs = _ffn(h2, x1, w_up[l].astype(BF16), conv_w[l], conv_b[l], w_down[l].astype(BF16),
                  g_ffn_post[l], tm=512, tf=512)
    return xs.reshape(bn, s_len, d)
```

```python
import functools
import math

import jax
import jax.numpy as jnp
from jax import lax
from jax.experimental import pallas as pl
from jax.experimental.pallas import tpu as pltpu

F32 = jnp.float32
BF16 = jnp.bfloat16
U32 = jnp.uint32

HEAD_DIM = 128
BLOCK = 128
A_HEADS, A_KV_HEADS, A_GROUP = 6, 2, 3
A_WINDOW = 128
B_HEADS = 6
B_PATTERNS = ((128, 1), (512, 4), (2048, 16))
M_HEADS = 4
REL_BUCKETS, REL_MAX_EXACT, REL_MAX_DISTANCE = 32, 16, 2048
CONV_WIDTH = 3
EPS = 1e-6
NEG = -1e30

A_Q = A_HEADS * HEAD_DIM
A_KV = A_KV_HEADS * HEAD_DIM
B_W = B_HEADS * HEAD_DIM
M_W = M_HEADS * HEAD_DIM
B_QW = B_W // 2
LOG2E = math.log2(math.e)
Q_SCALE = HEAD_DIM ** -0.5 * LOG2E

VMEM_LIMIT_BYTES = 56 * 1024 * 1024
SUBLANES = 8


def _params(n_axes):
    return pltpu.CompilerParams(dimension_semantics=("arbitrary",) * n_axes,
                                vmem_limit_bytes=VMEM_LIMIT_BYTES)


def _rms(x, g):
    return x * lax.rsqrt(jnp.mean(x * x, axis=-1, keepdims=True) + EPS) * g


def _dot_nt(a, b):
    return lax.dot_general(a, b, (((1,), (1,)), ((), ())), preferred_element_type=F32)


def _softmax_parts(s, sink=None):
    m = jnp.max(s, axis=-1, keepdims=True)
    if sink is not None:
        m = jnp.maximum(m, sink)
    p = jnp.exp2(s - m)
    den = jnp.sum(p, axis=-1, keepdims=True)
    if sink is not None:
        den = den + jnp.exp2(sink - m)
    return p, m, den


def _pack_pair(hi, lo):
    hi_bits = lax.bitcast_convert_type(hi.astype(BF16).astype(F32), U32)
    lo_bits = lax.bitcast_convert_type(lo.astype(BF16).astype(F32), U32)
    return hi_bits | (lo_bits >> 16)


def _unpack_hi(words):
    return lax.bitcast_convert_type(words & jnp.uint32(0xFFFF0000), F32).astype(BF16)


def _unpack_lo(words):
    return lax.bitcast_convert_type(words << 16, F32).astype(BF16)


def _proj_kernel(x_ref, g_ref, w_ref, *o_refs, outs):
    h = _rms(x_ref[...], g_ref[...]).astype(BF16)

    def cols(off, width, scale):
        acc = jnp.dot(h, w_ref[:, off:off + width], preferred_element_type=F32)
        return acc if scale == 1.0 else acc * scale

    for o_ref, (offs, width, scale) in zip(o_refs, outs):
        if len(offs) == 1:
            o_ref[...] = cols(offs[0], width, scale).astype(BF16)
        else:
            o_ref[...] = _pack_pair(cols(offs[0], width, scale), cols(offs[1], width, scale))


def _norm_proj(x, g, w, layer, outs, tm):
    rows, d = x.shape
    n = w.shape[2]
    assert rows % tm == 0
    return pl.pallas_call(
        functools.partial(_proj_kernel, outs=outs),
        grid=(rows // tm,),
        in_specs=[
            pl.BlockSpec((tm, d), lambda i: (i, 0)),
            pl.BlockSpec((1, d), lambda i: (0, 0)),
            pl.BlockSpec((None, d, n), lambda i: (layer, 0, 0), pipeline_mode=pl.Buffered(1)),
        ],
        out_specs=[pl.BlockSpec((tm, width), lambda i: (i, 0)) for _, width, _ in outs],
        out_shape=[jax.ShapeDtypeStruct((rows, width), BF16 if len(offs) == 1 else U32)
                   for offs, width, _ in outs],
        compiler_params=_params(1),
        name="norm_proj",
    )(x, g.reshape(1, d), w)


def _lookup_bias(table_ref, bucket, head):
    def body(b, acc):
        return jnp.where(bucket == b, table_ref[b, head] * LOG2E, acc)
    return lax.fori_loop(0, REL_BUCKETS, body, jnp.full(bucket.shape, NEG, F32))


def _attn_am_kernel(sink_ref, table_ref, bucket_ref, qa_ref, ka_ref, kap_ref, va_ref, vap_ref,
                    qm_ref, mk_ref, mv_ref, ga_ref, gm_ref, ya_ref, ym_ref, bias_scr, *, nsub):
    i = pl.program_id(0)

    @pl.when(i == 0)
    def _():
        for var in range(2):
            for h in range(A_HEADS):
                kv, g = divmod(h, A_GROUP)
                bias_scr[var, kv, g * BLOCK:(g + 1) * BLOCK, :] = _lookup_bias(
                    table_ref, bucket_ref[var], h)

    for sb in range(nsub):
        rows = slice(sb * BLOCK, (sb + 1) * BLOCK)
        prev_rows = slice((sb - 1) * BLOCK, sb * BLOCK)
        b_ref = bias_scr.at[jnp.minimum(i, 1) if sb == 0 else 1]
        outs = []
        for kv in range(A_KV_HEADS):
            cols = slice(kv * HEAD_DIM, (kv + 1) * HEAD_DIM)
            if sb == 0:
                k_prev, v_prev = kap_ref[:, cols], vap_ref[:, cols]
            else:
                k_prev, v_prev = ka_ref[prev_rows, cols], va_ref[prev_rows, cols]
            kk = jnp.concatenate([k_prev, ka_ref[rows, cols]], axis=0)
            vv = jnp.concatenate([v_prev, va_ref[rows, cols]], axis=0)
            heads = [kv * A_GROUP + g for g in range(A_GROUP)]
            q3 = jnp.concatenate(
                [qa_ref[rows, h * HEAD_DIM:(h + 1) * HEAD_DIM] for h in heads], axis=0)
            s = _dot_nt(q3, kk) + b_ref[kv]
            sink = jnp.concatenate(
                [jnp.full((BLOCK, 1), sink_ref[h] * LOG2E, F32) for h in heads], axis=0)
            p, _, den = _softmax_parts(s, sink)
            o = jnp.dot(p.astype(BF16), vv, preferred_element_type=F32) / den
            outs += [o[g * BLOCK:(g + 1) * BLOCK] for g in range(A_GROUP)]
        oa = jnp.concatenate(outs, axis=1)
        ya_ref[rows, :] = _rms(oa, ga_ref[...]).astype(BF16)

    outs = []
    for h in range(M_HEADS):
        cols = slice(h * HEAD_DIM, (h + 1) * HEAD_DIM)
        s = _dot_nt(qm_ref[:, cols], mk_ref[:, cols])
        p, _, den = _softmax_parts(s)
        outs.append(jnp.dot(p.astype(BF16), mv_ref[:, cols], preferred_element_type=F32) / den)
    ym_ref[...] = _rms(jnp.concatenate(outs, axis=1), gm_ref[...]).astype(BF16)


def _attn_am(qa, ka, va, qm, mk, mv, sinks, table, buckets, g_a, g_m, tq):
    s_len = qa.shape[0]
    nsub = tq // BLOCK
    mem_len = mk.shape[0]
    prev = lambda i: (jnp.maximum(i * nsub - 1, 0), 0)
    return pl.pallas_call(
        functools.partial(_attn_am_kernel, nsub=nsub),
        grid=(s_len // tq,),
        in_specs=[
            pl.BlockSpec(memory_space=pltpu.SMEM),
            pl.BlockSpec(memory_space=pltpu.SMEM),
            pl.BlockSpec(buckets.shape, lambda i: (0, 0, 0)),
            pl.BlockSpec((tq, A_Q), lambda i: (i, 0)),
            pl.BlockSpec((tq, A_KV), lambda i: (i, 0)),
            pl.BlockSpec((BLOCK, A_KV), prev),
            pl.BlockSpec((tq, A_KV), lambda i: (i, 0)),
            pl.BlockSpec((BLOCK, A_KV), prev),
            pl.BlockSpec((tq, M_W), lambda i: (i, 0)),
            pl.BlockSpec((mem_len, M_W), lambda i: (0, 0)),
            pl.BlockSpec((mem_len, M_W), lambda i: (0, 0)),
            pl.BlockSpec((1, A_Q), lambda i: (0, 0)),
            pl.BlockSpec((1, M_W), lambda i: (0, 0)),
        ],
        out_specs=[pl.BlockSpec((tq, A_Q), lambda i: (i, 0)),
                   pl.BlockSpec((tq, M_W), lambda i: (i, 0))],
        out_shape=[jax.ShapeDtypeStruct((s_len, A_Q), BF16),
                   jax.ShapeDtypeStruct((s_len, M_W), BF16)],
        scratch_shapes=[pltpu.VMEM((2, A_KV_HEADS, A_GROUP * BLOCK, 2 * BLOCK), F32)],
        compiler_params=_params(1),
        name="attn_am",
    )(sinks, table, buckets, qa, ka, ka, va, va, qm, mk, mv,
      g_a.reshape(1, A_Q), g_m.reshape(1, M_W))


def _build_bias_b(table_ref, bucket_ref, bias_scr):
    for var in range(2):
        for h in range(B_HEADS):
            bias_scr[var, h] = _lookup_bias(table_ref, bucket_ref[var], A_HEADS + h)


def _attn_b_tiles(qw_ref, kvw_ref, kvw_prev_ref, bias_scr, first_var, o_ref, lse_ref, nsub):
    ns = qw_ref.shape[0]
    per = BLOCK // ns
    half = B_HEADS // 2
    lane = lax.broadcasted_iota(jnp.int32, (BLOCK, HEAD_DIM), 1)

    def block(ref, sb, cols):
        return jnp.concatenate([ref[st, sb * per:(sb + 1) * per, cols] for st in range(ns)], axis=0)

    for sb in range(nsub):
        b_ref = bias_scr.at[first_var if sb == 0 else 1]
        lse_tile = jnp.zeros((BLOCK, HEAD_DIM), F32)
        for h in range(B_HEADS):
            cols = slice(h * HEAD_DIM, (h + 1) * HEAD_DIM)
            qw = block(qw_ref, sb, slice((h % half) * HEAD_DIM, (h % half + 1) * HEAD_DIM))
            q = _unpack_hi(qw) if h < half else _unpack_lo(qw)
            w_prev = block(kvw_prev_ref, 0, cols) if sb == 0 else block(kvw_ref, sb - 1, cols)
            w_cur = block(kvw_ref, sb, cols)
            kk = jnp.concatenate([_unpack_hi(w_prev), _unpack_hi(w_cur)], axis=0)
            vv = jnp.concatenate([_unpack_lo(w_prev), _unpack_lo(w_cur)], axis=0)
            s = _dot_nt(q, kk) + b_ref[h]
            p, m, den = _softmax_parts(s)
            o = jnp.dot(p.astype(BF16), vv, preferred_element_type=F32) / den
            for st in range(ns):
                o_ref[st, sb * per:(sb + 1) * per, cols] = o[st * per:(st + 1) * per]
            lse_tile = jnp.where(lane == h, m + jnp.log2(den), lse_tile)
        for st in range(ns):
            lse_ref[st, sb * per:(sb + 1) * per, :] = lse_tile[st * per:(st + 1) * per]


def _attn_b_nat_kernel(table_ref, bucket_ref, qw_ref, kvw_ref, kvwp_ref, o_ref, lse_ref, bias_scr,
                       *, nsub):
    i = pl.program_id(0)

    @pl.when(i == 0)
    def _():
        _build_bias_b(table_ref, bucket_ref, bias_scr)

    _attn_b_tiles(qw_ref, kvw_ref, kvwp_ref, bias_scr, jnp.minimum(i, 1), o_ref, lse_ref, nsub)


def _attn_b_dil_kernel(table_ref, bucket_ref, qw_hbm, kvw_hbm, o_hbm, lse_hbm,
                       qbuf, kvbuf, obuf, lbuf, carry, bias_scr, in_sems, out_sems,
                       *, nsub, steps_per_class, dil):
    s, n_steps = pl.program_id(0), pl.num_programs(0)
    ns = qbuf.shape[1]
    rows_per = qbuf.shape[2]
    slot = lax.rem(s, 2)

    def class_rows(step):
        c, i = lax.div(step, steps_per_class), lax.rem(step, steps_per_class)
        return [(pl.ds(i * rows_per, rows_per), c + st * dil) for st in range(ns)]

    def in_copies(step, sl):
        return [pltpu.make_async_copy(hbm.at[rows, col, :], buf.at[sl, st], in_sems.at[a, st, sl])
                for a, (hbm, buf) in enumerate(((qw_hbm, qbuf), (kvw_hbm, kvbuf)))
                for st, (rows, col) in enumerate(class_rows(step))]

    def out_copies(step, sl):
        return [pltpu.make_async_copy(buf.at[sl, st], hbm.at[rows, col, :], out_sems.at[a, st, sl])
                for a, (hbm, buf) in enumerate(((o_hbm, obuf), (lse_hbm, lbuf)))
                for st, (rows, col) in enumerate(class_rows(step))]

    @pl.when(s == 0)
    def _():
        for cp in in_copies(0, 0):
            cp.start()
        _build_bias_b(table_ref, bucket_ref, bias_scr)
        carry[...] = jnp.zeros(carry.shape, U32)

    @pl.when(s + 1 < n_steps)
    def _():
        for cp in in_copies(s + 1, 1 - slot):
            cp.start()

    for cp in in_copies(s, slot):
        cp.wait()

    @pl.when(s >= 2)
    def _():
        for cp in out_copies(s - 2, slot):
            cp.wait()

    first_var = jnp.minimum(lax.rem(s, steps_per_class), 1)
    _attn_b_tiles(qbuf.at[slot], kvbuf.at[slot], carry, bias_scr, first_var,
                  obuf.at[slot], lbuf.at[slot], nsub)
    carry[...] = kvbuf[slot, :, rows_per - carry.shape[1]:, :]

    for cp in out_copies(s, slot):
        cp.start()

    @pl.when(s == n_steps - 1)
    def _():
        for cp in out_copies(s - 1, 1 - slot) + out_copies(s, slot):
            cp.wait()


def _attn_b_nat(qw, kvw, table, buckets, tq):
    s_len = qw.shape[0]
    nsub = tq // BLOCK
    cur = lambda i: (0, i, 0)
    prev = lambda i: (0, jnp.maximum(i * nsub - 1, 0), 0)
    o, lse = pl.pallas_call(
        functools.partial(_attn_b_nat_kernel, nsub=nsub),
        grid=(s_len // tq,),
        in_specs=[
            pl.BlockSpec(memory_space=pltpu.SMEM),
            pl.BlockSpec(buckets.shape, lambda i: (0, 0, 0)),
            pl.BlockSpec((1, tq, B_QW), cur),
            pl.BlockSpec((1, tq, B_W), cur),
            pl.BlockSpec((1, BLOCK, B_W), prev),
        ],
        out_specs=[pl.BlockSpec((1, tq, B_W), cur), pl.BlockSpec((1, tq, HEAD_DIM), cur)],
        out_shape=[jax.ShapeDtypeStruct((1, s_len, B_W), F32),
                   jax.ShapeDtypeStruct((1, s_len, HEAD_DIM), F32)],
        scratch_shapes=[pltpu.VMEM((2, B_HEADS, BLOCK, 2 * BLOCK), F32)],
        compiler_params=_params(1),
        name="attn_b_d1",
    )(table, buckets, qw[None], kvw[None], kvw[None])
    return o[0], lse[0]


def _streams(dil):
    return 1 if dil == 1 else max(1, 8 // dil)


def _attn_b_dil(qw, kvw, table, buckets, dil, tq):
    s_len = qw.shape[0]
    ns = _streams(dil)
    n = dil * ns
    rows_per = tq // ns
    steps_per_class = s_len // dil // tq
    view = lambda t: t.reshape(s_len // n, n, t.shape[-1])
    any_spec = pl.BlockSpec(memory_space=pl.ANY)
    o, lse = pl.pallas_call(
        functools.partial(_attn_b_dil_kernel, nsub=tq // BLOCK, steps_per_class=steps_per_class,
                          dil=dil),
        grid=(dil * steps_per_class,),
        in_specs=[pl.BlockSpec(memory_space=pltpu.SMEM),
                  pl.BlockSpec(buckets.shape, lambda s: (0, 0, 0)), any_spec, any_spec],
        out_specs=[any_spec, any_spec],
        out_shape=[jax.ShapeDtypeStruct((s_len // n, n, B_W), F32),
                   jax.ShapeDtypeStruct((s_len // n, n, HEAD_DIM), F32)],
        scratch_shapes=[
            pltpu.VMEM((2, ns, rows_per, B_QW), U32),
            pltpu.VMEM((2, ns, rows_per, B_W), U32),
            pltpu.VMEM((2, ns, rows_per, B_W), F32),
            pltpu.VMEM((2, ns, rows_per, HEAD_DIM), F32),
            pltpu.VMEM((ns, BLOCK // ns, B_W), U32),
            pltpu.VMEM((2, B_HEADS, BLOCK, 2 * BLOCK), F32),
            pltpu.SemaphoreType.DMA((2, ns, 2)),
            pltpu.SemaphoreType.DMA((2, ns, 2)),
        ],
        compiler_params=_params(1),
        name=f"attn_b_d{dil}",
    )(table, buckets, view(qw), view(kvw))
    return o.reshape(s_len, B_W), lse.reshape(s_len, HEAD_DIM)


def _out_kernel(x_ref, ya_ref, ym_ref, o1_ref, o2_ref, o3_ref, l1_ref, l2_ref, l3_ref,
                gb_ref, w_ref, gpost_ref, gpre_ref, x1_ref, h2_ref, *, chunk):
    o_refs = (o1_ref, o2_ref, o3_ref)
    for c in range(x_ref.shape[0] // chunk):
        rows = slice(c * chunk, (c + 1) * chunk)
        obs = []
        for h in range(B_HEADS):
            cols = slice(h * HEAD_DIM, (h + 1) * HEAD_DIM)
            ls = [l_ref[rows, h:h + 1] for l_ref in (l1_ref, l2_ref, l3_ref)]
            mx = jnp.maximum(jnp.maximum(ls[0], ls[1]), ls[2])
            es = [jnp.exp2(l - mx) for l in ls]
            inv = 1.0 / (es[0] + es[1] + es[2])
            obs.append(sum((e * inv) * o_ref[rows, cols] for e, o_ref in zip(es, o_refs)))
        yb = _rms(jnp.concatenate(obs, axis=1), gb_ref[...]).astype(BF16)
        y_in = jnp.concatenate([ya_ref[rows, :], yb, ym_ref[rows, :]], axis=1)
        y = jnp.dot(y_in, w_ref[...], preferred_element_type=F32)
        x1 = x_ref[rows, :] + _rms(y, gpost_ref[...])
        x1_ref[rows, :] = x1
        h2_ref[rows, :] = _rms(x1, gpre_ref[...]).astype(BF16)


def _out_proj(x, ya, ym, obs, lses, g_b, w_out, layer, g_post, g_pre, tm, chunk):
    s_len, d = x.shape
    row = lambda width: pl.BlockSpec((tm, width), lambda i: (i, 0))
    vec = lambda width: pl.BlockSpec((1, width), lambda i: (0, 0))
    return pl.pallas_call(
        functools.partial(_out_kernel, chunk=chunk),
        grid=(s_len // tm,),
        in_specs=[row(d), row(A_Q), row(M_W), row(B_W), row(B_W), row(B_W),
                  row(HEAD_DIM), row(HEAD_DIM), row(HEAD_DIM), vec(B_W),
                  pl.BlockSpec((None,) + w_out.shape[1:], lambda i: (layer, 0, 0),
                               pipeline_mode=pl.Buffered(1)),
                  vec(d), vec(d)],
        out_specs=[row(d), row(d)],
        out_shape=[jax.ShapeDtypeStruct((s_len, d), F32), jax.ShapeDtypeStruct((s_len, d), BF16)],
        compiler_params=_params(1),
        name="out_proj",
    )(x, ya, ym, *obs, *lses, g_b.reshape(1, B_W), w_out, g_post.reshape(1, d),
      g_pre.reshape(1, d))


def _shift_rows(u, carry, shift):
    rows, n = u.shape
    above = jnp.concatenate([carry, u[:-SUBLANES]], axis=0).reshape(rows // SUBLANES, SUBLANES, n)
    from_above = lax.broadcasted_iota(jnp.int32, (1, SUBLANES, n), 1) >= SUBLANES - shift
    mixed = jnp.where(from_above, above, u.reshape(rows // SUBLANES, SUBLANES, n))
    return pltpu.roll(mixed, shift, axis=1).reshape(rows, n)


def _causal_conv(u, carry, cw, cb):
    return (cw[0:1] * _shift_rows(u, carry, 2) + cw[1:2] * _shift_rows(u, carry, 1)
            + cw[2:3] * u + cb)


def _ffn_kernel(h_ref, wg_ref, wv_ref, cwg_ref, cwv_ref, cbg_ref, cbv_ref, wd_ref, x_ref, g_ref,
                o_ref, carry_g, carry_v, *, chunks):
    i, j = pl.program_id(0), pl.program_id(1)

    @pl.when(i == 0)
    def _():
        carry_g[j] = jnp.zeros(carry_g.shape[1:], F32)
        carry_v[j] = jnp.zeros(carry_v.shape[1:], F32)

    @pl.when(j == 0)
    def _():
        o_ref[...] = jnp.zeros(o_ref.shape, F32)

    prev_g, prev_v = carry_g[j], carry_v[j]
    starts = [sum(chunks[:c]) for c in range(len(chunks))]
    for start, size in zip(starts, chunks):
        rows = slice(start, start + size)
        h = h_ref[rows, :]
        ug = jnp.dot(h, wg_ref[...], preferred_element_type=F32)
        uv = jnp.dot(h, wv_ref[...], preferred_element_type=F32)
        cg = _causal_conv(ug, prev_g, cwg_ref[...], cbg_ref[...])
        cv = _causal_conv(uv, prev_v, cwv_ref[...], cbv_ref[...])
        prev_g, prev_v = ug[-8:], uv[-8:]
        act = (jax.nn.gelu(cg, approximate=True) * cv).astype(BF16)
        o_ref[rows, :] += jnp.dot(act, wd_ref[...], preferred_element_type=F32)
    carry_g[j] = prev_g
    carry_v[j] = prev_v

    @pl.when(j == pl.num_programs(1) - 1)
    def _():
        o_ref[...] = x_ref[...] + _rms(o_ref[...], g_ref[...])


def _ffn(h2, x1, w_up, conv_w, conv_b, w_down, g_post, layer, tf, chunks):
    s_len, d = x1.shape
    d_ff = w_down.shape[1]
    nf = d_ff // tf
    tm = sum(chunks)
    assert d_ff % tf == 0 and s_len % tm == 0
    return pl.pallas_call(
        functools.partial(_ffn_kernel, chunks=chunks),
        grid=(s_len // tm, nf),
        in_specs=[
            pl.BlockSpec((tm, d), lambda i, j: (i, 0)),
            pl.BlockSpec((None, d, tf), lambda i, j: (layer, 0, j)),
            pl.BlockSpec((None, d, tf), lambda i, j: (layer, 0, nf + j)),
            pl.BlockSpec((CONV_WIDTH, tf), lambda i, j: (0, j)),
            pl.BlockSpec((CONV_WIDTH, tf), lambda i, j: (0, nf + j)),
            pl.BlockSpec((1, tf), lambda i, j: (0, j)),
            pl.BlockSpec((1, tf), lambda i, j: (0, nf + j)),
            pl.BlockSpec((None, tf, d), lambda i, j: (layer, j, 0)),
            pl.BlockSpec((tm, d), lambda i, j: (i, 0), pipeline_mode=pl.Buffered(1)),
            pl.BlockSpec((1, d), lambda i, j: (0, 0)),
        ],
        out_specs=pl.BlockSpec((tm, d), lambda i, j: (i, 0)),
        out_shape=jax.ShapeDtypeStruct((s_len, d), F32),
        scratch_shapes=[pltpu.VMEM((nf, 8, tf), F32), pltpu.VMEM((nf, 8, tf), F32)],
        compiler_params=_params(2),
        name="conv_glu_ffn",
    )(h2, w_up, w_up, conv_w, conv_w, conv_b.reshape(1, -1), conv_b.reshape(1, -1), w_down,
      x1, g_post.reshape(1, d))


def _rel_bucket(dist):
    n = jnp.maximum(dist, 0)
    nf = jnp.maximum(n, 1).astype(F32)
    large = REL_MAX_EXACT + (jnp.log(nf / REL_MAX_EXACT)
                             / math.log(REL_MAX_DISTANCE / REL_MAX_EXACT)
                             * (REL_BUCKETS - REL_MAX_EXACT)).astype(jnp.int32)
    large = jnp.minimum(large, REL_BUCKETS - 1)
    return jnp.where(n < REL_MAX_EXACT, n, large)


def _bucket_blocks(dilation, max_dist, streams=1):
    per = BLOCK // streams
    idx = jnp.arange(BLOCK)
    member = streams * (idx % per) + idx // per
    key = jnp.concatenate([member, BLOCK + member])
    rel = member[:, None] + BLOCK - key[None, :]
    valid = (rel >= 0) & (rel <= max_dist)
    normal = jnp.where(valid, _rel_bucket(rel * dilation), -1)
    first = jnp.where((jnp.arange(2 * BLOCK) >= BLOCK)[None, :], normal, -1)
    return jnp.stack([first, normal]).astype(jnp.int32)


def kernel(x, mem, rel_bias, g_mix_pre, w_in, a_sinks, g_mem, w_mem_kv, g_out, w_out, g_mix_post,
           g_ffn_pre, w_up, conv_w, conv_b, w_down, g_ffn_post):
    bn, s_len, d = x.shape
    assert bn == 1
    depth = w_in.shape[0]
    xs = x.reshape(s_len, d)
    mems = mem.reshape(mem.shape[1], d)

    buckets_a = _bucket_blocks(1, A_WINDOW - 1)
    buckets_b = [_bucket_blocks(dil, window // dil, _streams(dil)) for window, dil in B_PATTERNS]

    off_ka, off_va, off_qb = A_Q, A_Q + A_KV, A_Q + 2 * A_KV
    off_kb, off_vb, off_qm = off_qb + B_W, off_qb + 2 * B_W, off_qb + 3 * B_W
    proj_outs = (((0,), A_Q, Q_SCALE), ((off_ka,), A_KV, 1.0), ((off_va,), A_KV, 1.0),
                 ((off_qb, off_qb + B_QW), B_QW, Q_SCALE), ((off_kb, off_vb), B_W, 1.0),
                 ((off_qm,), M_W, Q_SCALE))
    mem_outs = (((0,), M_W, 1.0), ((M_W,), M_W, 1.0))

    w_in, w_mem_kv, w_out, w_up, w_down = (w.astype(BF16)
                                           for w in (w_in, w_mem_kv, w_out, w_up, w_down))

    for l in range(depth):
        ab = l % 2
        qa, ka, va, qbw, kvbw, qm = _norm_proj(xs, g_mix_pre[l], w_in, l, proj_outs, tm=512)
        mk, mv = _norm_proj(mems, g_mem[l], w_mem_kv, l, mem_outs, tm=mems.shape[0])
        ya, ym = _attn_am(qa, ka, va, qm, mk, mv, a_sinks[l], rel_bias, buckets_a,
                          g_out[l, :A_Q], g_out[l, A_Q + B_W:], tq=(1024, 2048)[ab])
        obs, lses = [], []
        for (window, dil), buckets in zip(B_PATTERNS, buckets_b):
            if dil == 1:
                o, lse = _attn_b_nat(qbw, kvbw, rel_bias, buckets, tq=(1024, 2048)[ab])
            else:
                o, lse = _attn_b_dil(qbw, kvbw, rel_bias, buckets, dil,
                                     tq=min((1024, 2048)[ab], s_len // dil))
            obs.append(o)
            lses.append(lse)
        x1, h2 = _out_proj(xs, ya, ym, obs, lses, g_out[l, A_Q:A_Q + B_W], w_out, l,
                           g_mix_post[l], g_ffn_pre[l], tm=512, chunk=128)
        xs = _ffn(h2, x1, w_up, conv_w[l], conv_b[l], w_down, g_ffn_post[l], l, tf=512,
                  chunks=((512, 512), (512, 256, 256))[ab])
    return xs.reshape(bn, s_len, d)
```

```python
import functools
import math

import jax
import jax.numpy as jnp
from jax import lax
from jax.experimental import pallas as pl
from jax.experimental.pallas import tpu as pltpu

F32 = jnp.float32
BF16 = jnp.bfloat16
U32 = jnp.uint32

HEAD_DIM = 128
BLOCK = 128
A_HEADS, A_KV_HEADS, A_GROUP = 6, 2, 3
A_WINDOW = 128
B_HEADS = 6
B_PATTERNS = ((128, 1), (512, 4), (2048, 16))
M_HEADS = 4
REL_BUCKETS, REL_MAX_EXACT, REL_MAX_DISTANCE = 32, 16, 2048
CONV_WIDTH = 3
EPS = 1e-6
NEG = -1e30

A_Q = A_HEADS * HEAD_DIM
A_KV = A_KV_HEADS * HEAD_DIM
B_W = B_HEADS * HEAD_DIM
M_W = M_HEADS * HEAD_DIM
B_QW = B_W // 2
LOG2E = math.log2(math.e)
Q_SCALE = HEAD_DIM ** -0.5 * LOG2E

VMEM_LIMIT_BYTES = 56 * 1024 * 1024
SUBLANES = 8


def _params(n_axes):
    return pltpu.CompilerParams(dimension_semantics=("arbitrary",) * n_axes,
                                vmem_limit_bytes=VMEM_LIMIT_BYTES)


def _rms(x, g):
    return x * lax.rsqrt(jnp.mean(x * x, axis=-1, keepdims=True) + EPS) * g


def _dot_nt(a, b):
    return lax.dot_general(a, b, (((1,), (1,)), ((), ())), preferred_element_type=F32)


def _softmax_parts(s, sink=None):
    m = jnp.max(s, axis=-1, keepdims=True)
    if sink is not None:
        m = jnp.maximum(m, sink)
    p = jnp.exp2(s - m)
    den = jnp.sum(p, axis=-1, keepdims=True)
    if sink is not None:
        den = den + jnp.exp2(sink - m)
    return p, m, den


def _pack_pair(hi, lo):
    hi_bits = lax.bitcast_convert_type(hi.astype(BF16).astype(F32), U32)
    lo_bits = lax.bitcast_convert_type(lo.astype(BF16).astype(F32), U32)
    return hi_bits | (lo_bits >> 16)


def _unpack_hi(words):
    return lax.bitcast_convert_type(words & jnp.uint32(0xFFFF0000), F32).astype(BF16)


def _unpack_lo(words):
    return lax.bitcast_convert_type(words << 16, F32).astype(BF16)


def _proj_kernel(x_ref, g_ref, w_ref, *refs, outs, n_cast):
    cast_in, o_refs = refs[:n_cast], refs[n_cast:n_cast + len(outs)]
    cast_out = refs[n_cast + len(outs):]
    h = _rms(x_ref[...], g_ref[...]).astype(BF16)

    def cols(off, width, scale):
        acc = jnp.dot(h, w_ref[:, off:off + width], preferred_element_type=F32)
        return acc if scale == 1.0 else acc * scale

    for o_ref, (offs, width, scale) in zip(o_refs, outs):
        if len(offs) == 1:
            o_ref[...] = cols(offs[0], width, scale).astype(BF16)
        else:
            o_ref[...] = _pack_pair(cols(offs[0], width, scale), cols(offs[1], width, scale))

    for src, dst in zip(cast_in, cast_out):
        dst[...] = src[...].astype(BF16)


def _norm_proj(x, g, w, outs, tm, cast=()):
    rows, d = x.shape
    n = w.shape[1]
    steps = rows // tm
    assert rows % tm == 0 and all(a.shape[1] % (16 * steps) == 0 for a, _ in cast)
    return pl.pallas_call(
        functools.partial(_proj_kernel, outs=outs, n_cast=len(cast)),
        grid=(steps,),
        in_specs=[
            pl.BlockSpec((tm, d), lambda i: (i, 0)),
            pl.BlockSpec((1, d), lambda i: (0, 0)),
            pl.BlockSpec((d, n), lambda i: (0, 0), pipeline_mode=pl.Buffered(1)),
        ] + [pl.BlockSpec((None, a.shape[1] // steps, a.shape[2]),
                          functools.partial(lambda layer, i: (layer, i, 0), layer))
             for a, layer in cast],
        out_specs=[pl.BlockSpec((tm, width), lambda i: (i, 0)) for _, width, _ in outs]
        + [pl.BlockSpec((a.shape[1] // steps, a.shape[2]), lambda i: (i, 0)) for a, _ in cast],
        out_shape=[jax.ShapeDtypeStruct((rows, width), BF16 if len(offs) == 1 else U32)
                   for offs, width, _ in outs]
        + [jax.ShapeDtypeStruct(a.shape[1:], BF16) for a, _ in cast],
        compiler_params=_params(1),
        name="norm_proj",
    )(x, g.reshape(1, d), w, *(a for a, _ in cast))


def _lookup_bias(table_ref, bucket, head):
    def body(b, acc):
        return jnp.where(bucket == b, table_ref[b, head] * LOG2E, acc)
    return lax.fori_loop(0, REL_BUCKETS, body, jnp.full(bucket.shape, NEG, F32))


def _attn_am_kernel(sink_ref, table_ref, bucket_ref, qa_ref, ka_ref, kap_ref, va_ref, vap_ref,
                    qm_ref, mk_ref, mv_ref, ga_ref, gm_ref, ya_ref, ym_ref, bias_scr, *, nsub):
    i = pl.program_id(0)

    @pl.when(i == 0)
    def _():
        for var in range(2):
            for h in range(A_HEADS):
                kv, g = divmod(h, A_GROUP)
                bias_scr[var, kv, g * BLOCK:(g + 1) * BLOCK, :] = _lookup_bias(
                    table_ref, bucket_ref[var], h)

    for sb in range(nsub):
        rows = slice(sb * BLOCK, (sb + 1) * BLOCK)
        prev_rows = slice((sb - 1) * BLOCK, sb * BLOCK)
        b_ref = bias_scr.at[jnp.minimum(i, 1) if sb == 0 else 1]
        outs = []
        for kv in range(A_KV_HEADS):
            cols = slice(kv * HEAD_DIM, (kv + 1) * HEAD_DIM)
            if sb == 0:
                k_prev, v_prev = kap_ref[:, cols], vap_ref[:, cols]
            else:
                k_prev, v_prev = ka_ref[prev_rows, cols], va_ref[prev_rows, cols]
            kk = jnp.concatenate([k_prev, ka_ref[rows, cols]], axis=0)
            vv = jnp.concatenate([v_prev, va_ref[rows, cols]], axis=0)
            heads = [kv * A_GROUP + g for g in range(A_GROUP)]
            q3 = jnp.concatenate(
                [qa_ref[rows, h * HEAD_DIM:(h + 1) * HEAD_DIM] for h in heads], axis=0)
            s = _dot_nt(q3, kk) + b_ref[kv]
            sink = jnp.concatenate(
                [jnp.full((BLOCK, 1), sink_ref[h] * LOG2E, F32) for h in heads], axis=0)
            p, _, den = _softmax_parts(s, sink)
            o = jnp.dot(p.astype(BF16), vv, preferred_element_type=F32) / den
            outs += [o[g * BLOCK:(g + 1) * BLOCK] for g in range(A_GROUP)]
        oa = jnp.concatenate(outs, axis=1)
        ya_ref[rows, :] = _rms(oa, ga_ref[...]).astype(BF16)

    outs = []
    for h in range(M_HEADS):
        cols = slice(h * HEAD_DIM, (h + 1) * HEAD_DIM)
        s = _dot_nt(qm_ref[:, cols], mk_ref[:, cols])
        p, _, den = _softmax_parts(s)
        outs.append(jnp.dot(p.astype(BF16), mv_ref[:, cols], preferred_element_type=F32) / den)
    ym_ref[...] = _rms(jnp.concatenate(outs, axis=1), gm_ref[...]).astype(BF16)


def _attn_am(qa, ka, va, qm, mk, mv, sinks, table, buckets, g_a, g_m, tq):
    s_len = qa.shape[0]
    nsub = tq // BLOCK
    mem_len = mk.shape[0]
    prev = lambda i: (jnp.maximum(i * nsub - 1, 0), 0)
    return pl.pallas_call(
        functools.partial(_attn_am_kernel, nsub=nsub),
        grid=(s_len // tq,),
        in_specs=[
            pl.BlockSpec(memory_space=pltpu.SMEM),
            pl.BlockSpec(memory_space=pltpu.SMEM),
            pl.BlockSpec(buckets.shape, lambda i: (0, 0, 0)),
            pl.BlockSpec((tq, A_Q), lambda i: (i, 0)),
            pl.BlockSpec((tq, A_KV), lambda i: (i, 0)),
            pl.BlockSpec((BLOCK, A_KV), prev),
            pl.BlockSpec((tq, A_KV), lambda i: (i, 0)),
            pl.BlockSpec((BLOCK, A_KV), prev),
            pl.BlockSpec((tq, M_W), lambda i: (i, 0)),
            pl.BlockSpec((mem_len, M_W), lambda i: (0, 0)),
            pl.BlockSpec((mem_len, M_W), lambda i: (0, 0)),
            pl.BlockSpec((1, A_Q), lambda i: (0, 0)),
            pl.BlockSpec((1, M_W), lambda i: (0, 0)),
        ],
        out_specs=[pl.BlockSpec((tq, A_Q), lambda i: (i, 0)),
                   pl.BlockSpec((tq, M_W), lambda i: (i, 0))],
        out_shape=[jax.ShapeDtypeStruct((s_len, A_Q), BF16),
                   jax.ShapeDtypeStruct((s_len, M_W), BF16)],
        scratch_shapes=[pltpu.VMEM((2, A_KV_HEADS, A_GROUP * BLOCK, 2 * BLOCK), F32)],
        compiler_params=_params(1),
        name="attn_am",
    )(sinks, table, buckets, qa, ka, ka, va, va, qm, mk, mv,
      g_a.reshape(1, A_Q), g_m.reshape(1, M_W))


def _build_bias_b(table_ref, bucket_ref, bias_scr):
    for var in range(2):
        for h in range(B_HEADS):
            bias_scr[var, h] = _lookup_bias(table_ref, bucket_ref[var], A_HEADS + h)


def _attn_b_tiles(qw_ref, kvw_ref, kvw_prev_ref, bias_scr, first_var, o_ref, lse_ref, nsub):
    ns = qw_ref.shape[0]
    per = BLOCK // ns
    half = B_HEADS // 2
    lane = lax.broadcasted_iota(jnp.int32, (BLOCK, HEAD_DIM), 1)

    def block(ref, sb, cols):
        return jnp.concatenate([ref[st, sb * per:(sb + 1) * per, cols] for st in range(ns)], axis=0)

    for sb in range(nsub):
        b_ref = bias_scr.at[first_var if sb == 0 else 1]
        lse_tile = jnp.zeros((BLOCK, HEAD_DIM), F32)
        for h in range(B_HEADS):
            cols = slice(h * HEAD_DIM, (h + 1) * HEAD_DIM)
            qw = block(qw_ref, sb, slice((h % half) * HEAD_DIM, (h % half + 1) * HEAD_DIM))
            q = _unpack_hi(qw) if h < half else _unpack_lo(qw)
            w_prev = block(kvw_prev_ref, 0, cols) if sb == 0 else block(kvw_ref, sb - 1, cols)
            w_cur = block(kvw_ref, sb, cols)
            kk = jnp.concatenate([_unpack_hi(w_prev), _unpack_hi(w_cur)], axis=0)
            vv = jnp.concatenate([_unpack_lo(w_prev), _unpack_lo(w_cur)], axis=0)
            s = _dot_nt(q, kk) + b_ref[h]
            p, m, den = _softmax_parts(s)
            o = jnp.dot(p.astype(BF16), vv, preferred_element_type=F32) / den
            for st in range(ns):
                o_ref[st, sb * per:(sb + 1) * per, cols] = o[st * per:(st + 1) * per]
            lse_tile = jnp.where(lane == h, m + jnp.log2(den), lse_tile)
        for st in range(ns):
            lse_ref[st, sb * per:(sb + 1) * per, :] = lse_tile[st * per:(st + 1) * per]


def _attn_b_nat_kernel(table_ref, bucket_ref, qw_ref, kvw_ref, kvwp_ref, o_ref, lse_ref, bias_scr,
                       *, nsub):
    i = pl.program_id(0)

    @pl.when(i == 0)
    def _():
        _build_bias_b(table_ref, bucket_ref, bias_scr)

    _attn_b_tiles(qw_ref, kvw_ref, kvwp_ref, bias_scr, jnp.minimum(i, 1), o_ref, lse_ref, nsub)


def _attn_b_dil_kernel(table_ref, bucket_ref, qw_hbm, kvw_hbm, o_hbm, lse_hbm,
                       qbuf, kvbuf, obuf, lbuf, carry, bias_scr, in_sems, out_sems,
                       *, nsub, steps_per_class, dil):
    s, n_steps = pl.program_id(0), pl.num_programs(0)
    ns = qbuf.shape[1]
    rows_per = qbuf.shape[2]
    slot = lax.rem(s, 2)

    def class_rows(step):
        c, i = lax.div(step, steps_per_class), lax.rem(step, steps_per_class)
        return [(pl.ds(i * rows_per, rows_per), c + st * dil) for st in range(ns)]

    def in_copies(step, sl):
        return [pltpu.make_async_copy(hbm.at[rows, col, :], buf.at[sl, st], in_sems.at[a, st, sl])
                for a, (hbm, buf) in enumerate(((qw_hbm, qbuf), (kvw_hbm, kvbuf)))
                for st, (rows, col) in enumerate(class_rows(step))]

    def out_copies(step, sl):
        return [pltpu.make_async_copy(buf.at[sl, st], hbm.at[rows, col, :], out_sems.at[a, st, sl])
                for a, (hbm, buf) in enumerate(((o_hbm, obuf), (lse_hbm, lbuf)))
                for st, (rows, col) in enumerate(class_rows(step))]

    @pl.when(s == 0)
    def _():
        for cp in in_copies(0, 0):
            cp.start()
        _build_bias_b(table_ref, bucket_ref, bias_scr)
        carry[...] = jnp.zeros(carry.shape, U32)

    @pl.when(s + 1 < n_steps)
    def _():
        for cp in in_copies(s + 1, 1 - slot):
            cp.start()

    for cp in in_copies(s, slot):
        cp.wait()

    @pl.when(s >= 2)
    def _():
        for cp in out_copies(s - 2, slot):
            cp.wait()

    first_var = jnp.minimum(lax.rem(s, steps_per_class), 1)
    _attn_b_tiles(qbuf.at[slot], kvbuf.at[slot], carry, bias_scr, first_var,
                  obuf.at[slot], lbuf.at[slot], nsub)
    carry[...] = kvbuf[slot, :, rows_per - carry.shape[1]:, :]

    for cp in out_copies(s, slot):
        cp.start()

    @pl.when(s == n_steps - 1)
    def _():
        for cp in out_copies(s - 1, 1 - slot) + out_copies(s, slot):
            cp.wait()


def _attn_b_nat(qw, kvw, table, buckets, tq):
    s_len = qw.shape[0]
    nsub = tq // BLOCK
    cur = lambda i: (0, i, 0)
    prev = lambda i: (0, jnp.maximum(i * nsub - 1, 0), 0)
    o, lse = pl.pallas_call(
        functools.partial(_attn_b_nat_kernel, nsub=nsub),
        grid=(s_len // tq,),
        in_specs=[
            pl.BlockSpec(memory_space=pltpu.SMEM),
            pl.BlockSpec(buckets.shape, lambda i: (0, 0, 0)),
            pl.BlockSpec((1, tq, B_QW), cur),
            pl.BlockSpec((1, tq, B_W), cur),
            pl.BlockSpec((1, BLOCK, B_W), prev),
        ],
        out_specs=[pl.BlockSpec((1, tq, B_W), cur), pl.BlockSpec((1, tq, HEAD_DIM), cur)],
        out_shape=[jax.ShapeDtypeStruct((1, s_len, B_W), F32),
                   jax.ShapeDtypeStruct((1, s_len, HEAD_DIM), F32)],
        scratch_shapes=[pltpu.VMEM((2, B_HEADS, BLOCK, 2 * BLOCK), F32)],
        compiler_params=_params(1),
        name="attn_b_d1",
    )(table, buckets, qw[None], kvw[None], kvw[None])
    return o[0], lse[0]


def _streams(dil):
    return 1 if dil == 1 else max(1, 8 // dil)


def _attn_b_dil(qw, kvw, table, buckets, dil, tq):
    s_len = qw.shape[0]
    ns = _streams(dil)
    n = dil * ns
    rows_per = tq // ns
    steps_per_class = s_len // dil // tq
    view = lambda t: t.reshape(s_len // n, n, t.shape[-1])
    any_spec = pl.BlockSpec(memory_space=pl.ANY)
    o, lse = pl.pallas_call(
        functools.partial(_attn_b_dil_kernel, nsub=tq // BLOCK, steps_per_class=steps_per_class,
                          dil=dil),
        grid=(dil * steps_per_class,),
        in_specs=[pl.BlockSpec(memory_space=pltpu.SMEM),
                  pl.BlockSpec(buckets.shape, lambda s: (0, 0, 0)), any_spec, any_spec],
        out_specs=[any_spec, any_spec],
        out_shape=[jax.ShapeDtypeStruct((s_len // n, n, B_W), F32),
                   jax.ShapeDtypeStruct((s_len // n, n, HEAD_DIM), F32)],
        scratch_shapes=[
            pltpu.VMEM((2, ns, rows_per, B_QW), U32),
            pltpu.VMEM((2, ns, rows_per, B_W), U32),
            pltpu.VMEM((2, ns, rows_per, B_W), F32),
            pltpu.VMEM((2, ns, rows_per, HEAD_DIM), F32),
            pltpu.VMEM((ns, BLOCK // ns, B_W), U32),
            pltpu.VMEM((2, B_HEADS, BLOCK, 2 * BLOCK), F32),
            pltpu.SemaphoreType.DMA((2, ns, 2)),
            pltpu.SemaphoreType.DMA((2, ns, 2)),
        ],
        compiler_params=_params(1),
        name=f"attn_b_d{dil}",
    )(table, buckets, view(qw), view(kvw))
    return o.reshape(s_len, B_W), lse.reshape(s_len, HEAD_DIM)


def _out_kernel(x_ref, ya_ref, ym_ref, o1_ref, o2_ref, o3_ref, l1_ref, l2_ref, l3_ref,
                gb_ref, w_ref, gpost_ref, gpre_ref, x1_ref, h2_ref, *, chunk):
    o_refs = (o1_ref, o2_ref, o3_ref)
    for c in range(x_ref.shape[0] // chunk):
        rows = slice(c * chunk, (c + 1) * chunk)
        obs = []
        for h in range(B_HEADS):
            cols = slice(h * HEAD_DIM, (h + 1) * HEAD_DIM)
            ls = [l_ref[rows, h:h + 1] for l_ref in (l1_ref, l2_ref, l3_ref)]
            mx = jnp.maximum(jnp.maximum(ls[0], ls[1]), ls[2])
            es = [jnp.exp2(l - mx) for l in ls]
            inv = 1.0 / (es[0] + es[1] + es[2])
            obs.append(sum((e * inv) * o_ref[rows, cols] for e, o_ref in zip(es, o_refs)))
        yb = _rms(jnp.concatenate(obs, axis=1), gb_ref[...]).astype(BF16)
        y_in = jnp.concatenate([ya_ref[rows, :], yb, ym_ref[rows, :]], axis=1)
        y = jnp.dot(y_in, w_ref[...], preferred_element_type=F32)
        x1 = x_ref[rows, :] + _rms(y, gpost_ref[...])
        x1_ref[rows, :] = x1
        h2_ref[rows, :] = _rms(x1, gpre_ref[...]).astype(BF16)


def _out_proj(x, ya, ym, obs, lses, g_b, w_out, g_post, g_pre, tm, chunk):
    s_len, d = x.shape
    row = lambda width: pl.BlockSpec((tm, width), lambda i: (i, 0))
    vec = lambda width: pl.BlockSpec((1, width), lambda i: (0, 0))
    return pl.pallas_call(
        functools.partial(_out_kernel, chunk=chunk),
        grid=(s_len // tm,),
        in_specs=[row(d), row(A_Q), row(M_W), row(B_W), row(B_W), row(B_W),
                  row(HEAD_DIM), row(HEAD_DIM), row(HEAD_DIM), vec(B_W),
                  pl.BlockSpec(w_out.shape, lambda i: (0, 0), pipeline_mode=pl.Buffered(1)),
                  vec(d), vec(d)],
        out_specs=[row(d), row(d)],
        out_shape=[jax.ShapeDtypeStruct((s_len, d), F32), jax.ShapeDtypeStruct((s_len, d), BF16)],
        compiler_params=_params(1),
        name="out_proj",
    )(x, ya, ym, *obs, *lses, g_b.reshape(1, B_W), w_out, g_post.reshape(1, d),
      g_pre.reshape(1, d))


def _shift_rows(u, carry, shift):
    rows, n = u.shape
    above = jnp.concatenate([carry, u[:-SUBLANES]], axis=0).reshape(rows // SUBLANES, SUBLANES, n)
    from_above = lax.broadcasted_iota(jnp.int32, (1, SUBLANES, n), 1) >= SUBLANES - shift
    mixed = jnp.where(from_above, above, u.reshape(rows // SUBLANES, SUBLANES, n))
    return pltpu.roll(mixed, shift, axis=1).reshape(rows, n)


def _causal_conv(u, carry, cw, cb):
    return (cw[0:1] * _shift_rows(u, carry, 2) + cw[1:2] * _shift_rows(u, carry, 1)
            + cw[2:3] * u + cb)


def _ffn_kernel(h_ref, wg_ref, wv_ref, cwg_ref, cwv_ref, cbg_ref, cbv_ref, wd_ref, x_ref, g_ref,
                o_ref, carry_g, carry_v, *, chunks):
    i, j = pl.program_id(0), pl.program_id(1)

    @pl.when(i == 0)
    def _():
        carry_g[j] = jnp.zeros(carry_g.shape[1:], F32)
        carry_v[j] = jnp.zeros(carry_v.shape[1:], F32)

    @pl.when(j == 0)
    def _():
        o_ref[...] = jnp.zeros(o_ref.shape, F32)

    prev_g, prev_v = carry_g[j], carry_v[j]
    starts = [sum(chunks[:c]) for c in range(len(chunks))]
    for start, size in zip(starts, chunks):
        rows = slice(start, start + size)
        h = h_ref[rows, :]
        ug = jnp.dot(h, wg_ref[...], preferred_element_type=F32)
        uv = jnp.dot(h, wv_ref[...], preferred_element_type=F32)
        cg = _causal_conv(ug, prev_g, cwg_ref[...], cbg_ref[...])
        cv = _causal_conv(uv, prev_v, cwv_ref[...], cbv_ref[...])
        prev_g, prev_v = ug[-8:], uv[-8:]
        act = (jax.nn.gelu(cg, approximate=True) * cv).astype(BF16)
        o_ref[rows, :] += jnp.dot(act, wd_ref[...], preferred_element_type=F32)
    carry_g[j] = prev_g
    carry_v[j] = prev_v

    @pl.when(j == pl.num_programs(1) - 1)
    def _():
        o_ref[...] = x_ref[...] + _rms(o_ref[...], g_ref[...])


def _ffn(h2, x1, w_up, conv_w, conv_b, w_down, g_post, tf, chunks):
    s_len, d = x1.shape
    d_ff = w_down.shape[0]
    nf = d_ff // tf
    tm = sum(chunks)
    assert d_ff % tf == 0 and s_len % tm == 0
    return pl.pallas_call(
        functools.partial(_ffn_kernel, chunks=chunks),
        grid=(s_len // tm, nf),
        in_specs=[
            pl.BlockSpec((tm, d), lambda i, j: (i, 0)),
            pl.BlockSpec((d, tf), lambda i, j: (0, j)),
            pl.BlockSpec((d, tf), lambda i, j: (0, nf + j)),
            pl.BlockSpec((CONV_WIDTH, tf), lambda i, j: (0, j)),
            pl.BlockSpec((CONV_WIDTH, tf), lambda i, j: (0, nf + j)),
            pl.BlockSpec((1, tf), lambda i, j: (0, j)),
            pl.BlockSpec((1, tf), lambda i, j: (0, nf + j)),
            pl.BlockSpec((tf, d), lambda i, j: (j, 0)),
            pl.BlockSpec((tm, d), lambda i, j: (i, 0), pipeline_mode=pl.Buffered(1)),
            pl.BlockSpec((1, d), lambda i, j: (0, 0)),
        ],
        out_specs=pl.BlockSpec((tm, d), lambda i, j: (i, 0)),
        out_shape=jax.ShapeDtypeStruct((s_len, d), F32),
        scratch_shapes=[pltpu.VMEM((nf, 8, tf), F32), pltpu.VMEM((nf, 8, tf), F32)],
        compiler_params=_params(2),
        name="conv_glu_ffn",
    )(h2, w_up, w_up, conv_w, conv_w, conv_b.reshape(1, -1), conv_b.reshape(1, -1), w_down,
      x1, g_post.reshape(1, d))


def _rel_bucket(dist):
    n = jnp.maximum(dist, 0)
    nf = jnp.maximum(n, 1).astype(F32)
    large = REL_MAX_EXACT + (jnp.log(nf / REL_MAX_EXACT)
                             / math.log(REL_MAX_DISTANCE / REL_MAX_EXACT)
                             * (REL_BUCKETS - REL_MAX_EXACT)).astype(jnp.int32)
    large = jnp.minimum(large, REL_BUCKETS - 1)
    return jnp.where(n < REL_MAX_EXACT, n, large)


def _bucket_blocks(dilation, max_dist, streams=1):
    per = BLOCK // streams
    idx = jnp.arange(BLOCK)
    member = streams * (idx % per) + idx // per
    key = jnp.concatenate([member, BLOCK + member])
    rel = member[:, None] + BLOCK - key[None, :]
    valid = (rel >= 0) & (rel <= max_dist)
    normal = jnp.where(valid, _rel_bucket(rel * dilation), -1)
    first = jnp.where((jnp.arange(2 * BLOCK) >= BLOCK)[None, :], normal, -1)
    return jnp.stack([first, normal]).astype(jnp.int32)


def kernel(x, mem, rel_bias, g_mix_pre, w_in, a_sinks, g_mem, w_mem_kv, g_out, w_out, g_mix_post,
           g_ffn_pre, w_up, conv_w, conv_b, w_down, g_ffn_post):
    bn, s_len, d = x.shape
    assert bn == 1
    depth = w_in.shape[0]
    xs = x.reshape(s_len, d)
    mems = mem.reshape(mem.shape[1], d)

    buckets_a = _bucket_blocks(1, A_WINDOW - 1)
    buckets_b = [_bucket_blocks(dil, window // dil, _streams(dil)) for window, dil in B_PATTERNS]

    off_ka, off_va, off_qb = A_Q, A_Q + A_KV, A_Q + 2 * A_KV
    off_kb, off_vb, off_qm = off_qb + B_W, off_qb + 2 * B_W, off_qb + 3 * B_W
    proj_outs = (((0,), A_Q, Q_SCALE), ((off_ka,), A_KV, 1.0), ((off_va,), A_KV, 1.0),
                 ((off_qb, off_qb + B_QW), B_QW, Q_SCALE), ((off_kb, off_vb), B_W, 1.0),
                 ((off_qm,), M_W, Q_SCALE))
    mem_outs = (((0,), M_W, 1.0), ((M_W,), M_W, 1.0))

    w_in_l = w_in[0].astype(BF16)
    tq = 2048

    for l in range(depth):
        cast = [(w_up, l), (w_down, l), (w_out, l)] + ([(w_in, l + 1)] if l + 1 < depth else [])
        qa, ka, va, qbw, kvbw, qm, w_up_l, w_down_l, w_out_l, *w_in_next = _norm_proj(
            xs, g_mix_pre[l], w_in_l, proj_outs, tm=512, cast=cast)
        w_in_l = w_in_next[0] if w_in_next else None
        mk, mv = _norm_proj(mems, g_mem[l], w_mem_kv[l].astype(BF16), mem_outs, tm=mems.shape[0])
        ya, ym = _attn_am(qa, ka, va, qm, mk, mv, a_sinks[l], rel_bias, buckets_a,
                          g_out[l, :A_Q], g_out[l, A_Q + B_W:], tq=tq)
        obs, lses = [], []
        for (window, dil), buckets in zip(B_PATTERNS, buckets_b):
            if dil == 1:
                o, lse = _attn_b_nat(qbw, kvbw, rel_bias, buckets, tq=tq)
            else:
                o, lse = _attn_b_dil(qbw, kvbw, rel_bias, buckets, dil, tq=min(tq, s_len // dil))
            obs.append(o)
            lses.append(lse)
        x1, h2 = _out_proj(xs, ya, ym, obs, lses, g_out[l, A_Q:A_Q + B_W], w_out_l,
                           g_mix_post[l], g_ffn_pre[l], tm=512, chunk=128)
        xs = _ffn(h2, x1, w_up_l, conv_w[l], conv_b[l], w_down_l, g_ffn_post[l], tf=512,
                  chunks=(512, 512))
    return xs.reshape(bn, s_len, d)
```

```python
import functools
import math

import jax
import jax.numpy as jnp
from jax import lax
from jax.experimental import pallas as pl
from jax.experimental.pallas import tpu as pltpu

F32 = jnp.float32
BF16 = jnp.bfloat16
U32 = jnp.uint32

HEAD_DIM = 128
BLOCK = 128
A_HEADS, A_KV_HEADS, A_GROUP = 6, 2, 3
A_WINDOW = 128
B_HEADS = 6
B_PATTERNS = ((128, 1), (512, 4), (2048, 16))
M_HEADS = 4
REL_BUCKETS, REL_MAX_EXACT, REL_MAX_DISTANCE = 32, 16, 2048
CONV_WIDTH = 3
EPS = 1e-6
NEG = -1e30

A_Q = A_HEADS * HEAD_DIM
A_KV = A_KV_HEADS * HEAD_DIM
B_W = B_HEADS * HEAD_DIM
M_W = M_HEADS * HEAD_DIM
B_QW = B_W // 2
LOG2E = math.log2(math.e)
Q_SCALE = HEAD_DIM ** -0.5 * LOG2E

VMEM_LIMIT_BYTES = 56 * 1024 * 1024
SUBLANES = 8


def _params(n_axes):
    return pltpu.CompilerParams(dimension_semantics=("arbitrary",) * n_axes,
                                vmem_limit_bytes=VMEM_LIMIT_BYTES)


def _rms(x, g):
    return x * lax.rsqrt(jnp.mean(x * x, axis=-1, keepdims=True) + EPS) * g


def _dot_nt(a, b):
    return lax.dot_general(a, b, (((1,), (1,)), ((), ())), preferred_element_type=F32)


def _softmax_parts(s, sink=None):
    m = jnp.max(s, axis=-1, keepdims=True)
    if sink is not None:
        m = jnp.maximum(m, sink)
    p = jnp.exp2(s - m)
    den = jnp.sum(p, axis=-1, keepdims=True)
    if sink is not None:
        den = den + jnp.exp2(sink - m)
    return p, m, den


def _pack_pair(hi, lo):
    hi_bits = lax.bitcast_convert_type(hi.astype(BF16).astype(F32), U32)
    lo_bits = lax.bitcast_convert_type(lo.astype(BF16).astype(F32), U32)
    return hi_bits | (lo_bits >> 16)


def _unpack_hi(words):
    return lax.bitcast_convert_type(words & jnp.uint32(0xFFFF0000), F32).astype(BF16)


def _unpack_lo(words):
    return lax.bitcast_convert_type(words << 16, F32).astype(BF16)


def _proj_kernel(x_ref, g_ref, w_ref, *refs, outs, n_cast):
    cast_in, o_refs = refs[:n_cast], refs[n_cast:n_cast + len(outs)]
    cast_out = refs[n_cast + len(outs):]
    h = _rms(x_ref[...], g_ref[...]).astype(BF16)

    def cols(off, width, scale):
        acc = jnp.dot(h, w_ref[:, off:off + width], preferred_element_type=F32)
        return acc if scale == 1.0 else acc * scale

    for o_ref, (offs, width, scale) in zip(o_refs, outs):
        if len(offs) == 1:
            o_ref[...] = cols(offs[0], width, scale).astype(BF16)
        else:
            o_ref[...] = _pack_pair(cols(offs[0], width, scale), cols(offs[1], width, scale))

    for src, dst in zip(cast_in, cast_out):
        dst[...] = src[...].astype(BF16)


def _norm_proj(x, g, w, outs, tm, cast=()):
    rows, d = x.shape
    n = w.shape[1]
    steps = rows // tm
    assert rows % tm == 0 and all(a.shape[1] % (16 * steps) == 0 for a, _ in cast)
    return pl.pallas_call(
        functools.partial(_proj_kernel, outs=outs, n_cast=len(cast)),
        grid=(steps,),
        in_specs=[
            pl.BlockSpec((tm, d), lambda i: (i, 0)),
            pl.BlockSpec((1, d), lambda i: (0, 0)),
            pl.BlockSpec((d, n), lambda i: (0, 0), pipeline_mode=pl.Buffered(1)),
        ] + [pl.BlockSpec((None, a.shape[1] // steps, a.shape[2]),
                          functools.partial(lambda layer, i: (layer, i, 0), layer))
             for a, layer in cast],
        out_specs=[pl.BlockSpec((tm, width), lambda i: (i, 0)) for _, width, _ in outs]
        + [pl.BlockSpec((a.shape[1] // steps, a.shape[2]), lambda i: (i, 0)) for a, _ in cast],
        out_shape=[jax.ShapeDtypeStruct((rows, width), BF16 if len(offs) == 1 else U32)
                   for offs, width, _ in outs]
        + [jax.ShapeDtypeStruct(a.shape[1:], BF16) for a, _ in cast],
        compiler_params=_params(1),
        name="norm_proj",
    )(x, g.reshape(1, d), w, *(a for a, _ in cast))


def _lookup_bias(table_ref, bucket, head):
    def body(b, acc):
        return jnp.where(bucket == b, table_ref[b, head] * LOG2E, acc)
    return lax.fori_loop(0, REL_BUCKETS, body, jnp.full(bucket.shape, NEG, F32))


def _attn_am_kernel(sink_ref, table_ref, bucket_ref, qa_ref, ka_ref, kap_ref, va_ref, vap_ref,
                    qm_ref, mk_ref, mv_ref, ga_ref, gm_ref, ya_ref, ym_ref, bias_scr, *, nsub):
    i = pl.program_id(0)

    @pl.when(i == 0)
    def _():
        for var in range(2):
            for h in range(A_HEADS):
                kv, g = divmod(h, A_GROUP)
                bias_scr[var, kv, g * BLOCK:(g + 1) * BLOCK, :] = _lookup_bias(
                    table_ref, bucket_ref[var], h)

    for sb in range(nsub):
        rows = slice(sb * BLOCK, (sb + 1) * BLOCK)
        prev_rows = slice((sb - 1) * BLOCK, sb * BLOCK)
        b_ref = bias_scr.at[jnp.minimum(i, 1) if sb == 0 else 1]
        outs = []
        for kv in range(A_KV_HEADS):
            cols = slice(kv * HEAD_DIM, (kv + 1) * HEAD_DIM)
            if sb == 0:
                k_prev, v_prev = kap_ref[:, cols], vap_ref[:, cols]
            else:
                k_prev, v_prev = ka_ref[prev_rows, cols], va_ref[prev_rows, cols]
            kk = jnp.concatenate([k_prev, ka_ref[rows, cols]], axis=0)
            vv = jnp.concatenate([v_prev, va_ref[rows, cols]], axis=0)
            heads = [kv * A_GROUP + g for g in range(A_GROUP)]
            q3 = jnp.concatenate(
                [qa_ref[rows, h * HEAD_DIM:(h + 1) * HEAD_DIM] for h in heads], axis=0)
            s = _dot_nt(q3, kk) + b_ref[kv]
            sink = jnp.concatenate(
                [jnp.full((BLOCK, 1), sink_ref[h] * LOG2E, F32) for h in heads], axis=0)
            p, _, den = _softmax_parts(s, sink)
            o = jnp.dot(p.astype(BF16), vv, preferred_element_type=F32) / den
            outs += [o[g * BLOCK:(g + 1) * BLOCK] for g in range(A_GROUP)]
        oa = jnp.concatenate(outs, axis=1)
        ya_ref[rows, :] = _rms(oa, ga_ref[...]).astype(BF16)

    outs = []
    for h in range(M_HEADS):
        cols = slice(h * HEAD_DIM, (h + 1) * HEAD_DIM)
        s = _dot_nt(qm_ref[:, cols], mk_ref[:, cols])
        p, _, den = _softmax_parts(s)
        outs.append(jnp.dot(p.astype(BF16), mv_ref[:, cols], preferred_element_type=F32) / den)
    ym_ref[...] = _rms(jnp.concatenate(outs, axis=1), gm_ref[...]).astype(BF16)


def _attn_am(qa, ka, va, qm, mk, mv, sinks, table, buckets, g_a, g_m, tq):
    s_len = qa.shape[0]
    nsub = tq // BLOCK
    mem_len = mk.shape[0]
    prev = lambda i: (jnp.maximum(i * nsub - 1, 0), 0)
    return pl.pallas_call(
        functools.partial(_attn_am_kernel, nsub=nsub),
        grid=(s_len // tq,),
        in_specs=[
            pl.BlockSpec(memory_space=pltpu.SMEM),
            pl.BlockSpec(memory_space=pltpu.SMEM),
            pl.BlockSpec(buckets.shape, lambda i: (0, 0, 0)),
            pl.BlockSpec((tq, A_Q), lambda i: (i, 0)),
            pl.BlockSpec((tq, A_KV), lambda i: (i, 0)),
            pl.BlockSpec((BLOCK, A_KV), prev),
            pl.BlockSpec((tq, A_KV), lambda i: (i, 0)),
            pl.BlockSpec((BLOCK, A_KV), prev),
            pl.BlockSpec((tq, M_W), lambda i: (i, 0)),
            pl.BlockSpec((mem_len, M_W), lambda i: (0, 0)),
            pl.BlockSpec((mem_len, M_W), lambda i: (0, 0)),
            pl.BlockSpec((1, A_Q), lambda i: (0, 0)),
            pl.BlockSpec((1, M_W), lambda i: (0, 0)),
        ],
        out_specs=[pl.BlockSpec((tq, A_Q), lambda i: (i, 0)),
                   pl.BlockSpec((tq, M_W), lambda i: (i, 0))],
        out_shape=[jax.ShapeDtypeStruct((s_len, A_Q), BF16),
                   jax.ShapeDtypeStruct((s_len, M_W), BF16)],
        scratch_shapes=[pltpu.VMEM((2, A_KV_HEADS, A_GROUP * BLOCK, 2 * BLOCK), F32)],
        compiler_params=_params(1),
        name="attn_am",
    )(sinks, table, buckets, qa, ka, ka, va, va, qm, mk, mv,
      g_a.reshape(1, A_Q), g_m.reshape(1, M_W))


def _build_bias_b(table_ref, bucket_ref, bias_scr):
    for var in range(2):
        for h in range(B_HEADS):
            bias_scr[var, h] = _lookup_bias(table_ref, bucket_ref[var], A_HEADS + h)


def _attn_b_tiles(qw_ref, kvw_ref, kvw_prev_ref, bias_scr, first_var, o_ref, lse_ref, nsub):
    ns = qw_ref.shape[0]
    per = BLOCK // ns
    half = B_HEADS // 2
    lane = lax.broadcasted_iota(jnp.int32, (BLOCK, HEAD_DIM), 1)

    def block(ref, sb, cols):
        return jnp.concatenate([ref[st, sb * per:(sb + 1) * per, cols] for st in range(ns)], axis=0)

    for sb in range(nsub):
        b_ref = bias_scr.at[first_var if sb == 0 else 1]
        lse_tile = jnp.zeros((BLOCK, HEAD_DIM), F32)
        for h in range(B_HEADS):
            cols = slice(h * HEAD_DIM, (h + 1) * HEAD_DIM)
            qw = block(qw_ref, sb, slice((h % half) * HEAD_DIM, (h % half + 1) * HEAD_DIM))
            q = _unpack_hi(qw) if h < half else _unpack_lo(qw)
            w_prev = block(kvw_prev_ref, 0, cols) if sb == 0 else block(kvw_ref, sb - 1, cols)
            w_cur = block(kvw_ref, sb, cols)
            kk = jnp.concatenate([_unpack_hi(w_prev), _unpack_hi(w_cur)], axis=0)
            vv = jnp.concatenate([_unpack_lo(w_prev), _unpack_lo(w_cur)], axis=0)
            s = _dot_nt(q, kk) + b_ref[h]
            p, m, den = _softmax_parts(s)
            o = jnp.dot(p.astype(BF16), vv, preferred_element_type=F32) / den
            for st in range(ns):
                o_ref[st, sb * per:(sb + 1) * per, cols] = o[st * per:(st + 1) * per]
            lse_tile = jnp.where(lane == h, m + jnp.log2(den), lse_tile)
        for st in range(ns):
            lse_ref[st, sb * per:(sb + 1) * per, :] = lse_tile[st * per:(st + 1) * per]


def _attn_b_nat_kernel(table_ref, bucket_ref, qw_ref, kvw_ref, kvwp_ref, o_ref, lse_ref, bias_scr,
                       *, nsub):
    i = pl.program_id(0)

    @pl.when(i == 0)
    def _():
        _build_bias_b(table_ref, bucket_ref, bias_scr)

    _attn_b_tiles(qw_ref, kvw_ref, kvwp_ref, bias_scr, jnp.minimum(i, 1), o_ref, lse_ref, nsub)


def _attn_b_dil_kernel(table_ref, bucket_ref, qw_hbm, kvw_hbm, o_hbm, lse_hbm,
                       qbuf, kvbuf, obuf, lbuf, carry, bias_scr, in_sems, out_sems,
                       *, nsub, steps_per_class, dil):
    s, n_steps = pl.program_id(0), pl.num_programs(0)
    ns = qbuf.shape[1]
    rows_per = qbuf.shape[2]
    slot = lax.rem(s, 2)

    def class_rows(step):
        c, i = lax.div(step, steps_per_class), lax.rem(step, steps_per_class)
        return [(pl.ds(i * rows_per, rows_per), c + st * dil) for st in range(ns)]

    def in_copies(step, sl):
        return [pltpu.make_async_copy(hbm.at[rows, col, :], buf.at[sl, st], in_sems.at[a, st, sl])
                for a, (hbm, buf) in enumerate(((qw_hbm, qbuf), (kvw_hbm, kvbuf)))
                for st, (rows, col) in enumerate(class_rows(step))]

    def out_copies(step, sl):
        return [pltpu.make_async_copy(buf.at[sl, st], hbm.at[rows, col, :], out_sems.at[a, st, sl])
                for a, (hbm, buf) in enumerate(((o_hbm, obuf), (lse_hbm, lbuf)))
                for st, (rows, col) in enumerate(class_rows(step))]

    @pl.when(s == 0)
    def _():
        for cp in in_copies(0, 0):
            cp.start()
        _build_bias_b(table_ref, bucket_ref, bias_scr)
        carry[...] = jnp.zeros(carry.shape, U32)

    @pl.when(s + 1 < n_steps)
    def _():
        for cp in in_copies(s + 1, 1 - slot):
            cp.start()

    for cp in in_copies(s, slot):
        cp.wait()

    @pl.when(s >= 2)
    def _():
        for cp in out_copies(s - 2, slot):
            cp.wait()

    first_var = jnp.minimum(lax.rem(s, steps_per_class), 1)
    _attn_b_tiles(qbuf.at[slot], kvbuf.at[slot], carry, bias_scr, first_var,
                  obuf.at[slot], lbuf.at[slot], nsub)
    carry[...] = kvbuf[slot, :, rows_per - carry.shape[1]:, :]

    for cp in out_copies(s, slot):
        cp.start()

    @pl.when(s == n_steps - 1)
    def _():
        for cp in out_copies(s - 1, 1 - slot) + out_copies(s, slot):
            cp.wait()


def _attn_b_nat(qw, kvw, table, buckets, tq):
    s_len = qw.shape[0]
    nsub = tq // BLOCK
    cur = lambda i: (0, i, 0)
    prev = lambda i: (0, jnp.maximum(i * nsub - 1, 0), 0)
    o, lse = pl.pallas_call(
        functools.partial(_attn_b_nat_kernel, nsub=nsub),
        grid=(s_len // tq,),
        in_specs=[
            pl.BlockSpec(memory_space=pltpu.SMEM),
            pl.BlockSpec(buckets.shape, lambda i: (0, 0, 0)),
            pl.BlockSpec((1, tq, B_QW), cur),
            pl.BlockSpec((1, tq, B_W), cur),
            pl.BlockSpec((1, BLOCK, B_W), prev),
        ],
        out_specs=[pl.BlockSpec((1, tq, B_W), cur), pl.BlockSpec((1, tq, HEAD_DIM), cur)],
        out_shape=[jax.ShapeDtypeStruct((1, s_len, B_W), F32),
                   jax.ShapeDtypeStruct((1, s_len, HEAD_DIM), F32)],
        scratch_shapes=[pltpu.VMEM((2, B_HEADS, BLOCK, 2 * BLOCK), F32)],
        compiler_params=_params(1),
        name="attn_b_d1",
    )(table, buckets, qw[None], kvw[None], kvw[None])
    return o[0], lse[0]


def _streams(dil):
    return 1 if dil == 1 else max(1, 8 // dil)


def _attn_b_dil(qw, kvw, table, buckets, dil, tq):
    s_len = qw.shape[0]
    ns = _streams(dil)
    n = dil * ns
    rows_per = tq // ns
    steps_per_class = s_len // dil // tq
    view = lambda t: t.reshape(s_len // n, n, t.shape[-1])
    any_spec = pl.BlockSpec(memory_space=pl.ANY)
    o, lse = pl.pallas_call(
        functools.partial(_attn_b_dil_kernel, nsub=tq // BLOCK, steps_per_class=steps_per_class,
                          dil=dil),
        grid=(dil * steps_per_class,),
        in_specs=[pl.BlockSpec(memory_space=pltpu.SMEM),
                  pl.BlockSpec(buckets.shape, lambda s: (0, 0, 0)), any_spec, any_spec],
        out_specs=[any_spec, any_spec],
        out_shape=[jax.ShapeDtypeStruct((s_len // n, n, B_W), F32),
                   jax.ShapeDtypeStruct((s_len // n, n, HEAD_DIM), F32)],
        scratch_shapes=[
            pltpu.VMEM((2, ns, rows_per, B_QW), U32),
            pltpu.VMEM((2, ns, rows_per, B_W), U32),
            pltpu.VMEM((2, ns, rows_per, B_W), F32),
            pltpu.VMEM((2, ns, rows_per, HEAD_DIM), F32),
            pltpu.VMEM((ns, BLOCK // ns, B_W), U32),
            pltpu.VMEM((2, B_HEADS, BLOCK, 2 * BLOCK), F32),
            pltpu.SemaphoreType.DMA((2, ns, 2)),
            pltpu.SemaphoreType.DMA((2, ns, 2)),
        ],
        compiler_params=_params(1),
        name=f"attn_b_d{dil}",
    )(table, buckets, view(qw), view(kvw))
    return o.reshape(s_len, B_W), lse.reshape(s_len, HEAD_DIM)


def _out_kernel(x_ref, ya_ref, ym_ref, o1_ref, o2_ref, o3_ref, l1_ref, l2_ref, l3_ref,
                gb_ref, w_ref, gpost_ref, gpre_ref, x1_ref, h2_ref, *, chunk):
    o_refs = (o1_ref, o2_ref, o3_ref)
    for c in range(x_ref.shape[0] // chunk):
        rows = slice(c * chunk, (c + 1) * chunk)
        obs = []
        for h in range(B_HEADS):
            cols = slice(h * HEAD_DIM, (h + 1) * HEAD_DIM)
            ls = [l_ref[rows, h:h + 1] for l_ref in (l1_ref, l2_ref, l3_ref)]
            mx = jnp.maximum(jnp.maximum(ls[0], ls[1]), ls[2])
            es = [jnp.exp2(l - mx) for l in ls]
            inv = 1.0 / (es[0] + es[1] + es[2])
            obs.append(sum((e * inv) * o_ref[rows, cols] for e, o_ref in zip(es, o_refs)))
        yb = _rms(jnp.concatenate(obs, axis=1), gb_ref[...]).astype(BF16)
        y_in = jnp.concatenate([ya_ref[rows, :], yb, ym_ref[rows, :]], axis=1)
        y = jnp.dot(y_in, w_ref[...], preferred_element_type=F32)
        x1 = x_ref[rows, :] + _rms(y, gpost_ref[...])
        x1_ref[rows, :] = x1
        h2_ref[rows, :] = _rms(x1, gpre_ref[...]).astype(BF16)


def _out_proj(x, ya, ym, obs, lses, g_b, w_out, g_post, g_pre, tm, chunk):
    s_len, d = x.shape
    row = lambda width: pl.BlockSpec((tm, width), lambda i: (i, 0))
    vec = lambda width: pl.BlockSpec((1, width), lambda i: (0, 0))
    return pl.pallas_call(
        functools.partial(_out_kernel, chunk=chunk),
        grid=(s_len // tm,),
        in_specs=[row(d), row(A_Q), row(M_W), row(B_W), row(B_W), row(B_W),
                  row(HEAD_DIM), row(HEAD_DIM), row(HEAD_DIM), vec(B_W),
                  pl.BlockSpec(w_out.shape, lambda i: (0, 0), pipeline_mode=pl.Buffered(1)),
                  vec(d), vec(d)],
        out_specs=[row(d), row(d)],
        out_shape=[jax.ShapeDtypeStruct((s_len, d), F32), jax.ShapeDtypeStruct((s_len, d), BF16)],
        compiler_params=_params(1),
        name="out_proj",
    )(x, ya, ym, *obs, *lses, g_b.reshape(1, B_W), w_out, g_post.reshape(1, d),
      g_pre.reshape(1, d))


def _shift_rows(u, carry, shift):
    rows, n = u.shape
    above = jnp.concatenate([carry, u[:-SUBLANES]], axis=0).reshape(rows // SUBLANES, SUBLANES, n)
    from_above = lax.broadcasted_iota(jnp.int32, (1, SUBLANES, n), 1) >= SUBLANES - shift
    mixed = jnp.where(from_above, above, u.reshape(rows // SUBLANES, SUBLANES, n))
    return pltpu.roll(mixed, shift, axis=1).reshape(rows, n)


def _causal_conv(u, carry, cw, cb):
    return (cw[0:1] * _shift_rows(u, carry, 2) + cw[1:2] * _shift_rows(u, carry, 1)
            + cw[2:3] * u + cb)


def _ffn_kernel(h_ref, wg_ref, wv_ref, cwg_ref, cwv_ref, cbg_ref, cbv_ref, wd_ref, x_hbm, g_ref,
                o_ref, carry_g, carry_v, x_buf, x_sem, *, chunks):
    i, j = pl.program_id(0), pl.program_id(1)
    tm = x_buf.shape[0]
    x_copy = pltpu.make_async_copy(x_hbm.at[pl.ds(i * tm, tm), :], x_buf, x_sem)

    @pl.when(i == 0)
    def _():
        carry_g[j] = jnp.zeros(carry_g.shape[1:], F32)
        carry_v[j] = jnp.zeros(carry_v.shape[1:], F32)

    @pl.when(j == 0)
    def _():
        x_copy.start()
        o_ref[...] = jnp.zeros(o_ref.shape, F32)

    prev_g, prev_v = carry_g[j], carry_v[j]
    starts = [sum(chunks[:c]) for c in range(len(chunks))]
    for start, size in zip(starts, chunks):
        rows = slice(start, start + size)
        h = h_ref[rows, :]
        ug = jnp.dot(h, wg_ref[...], preferred_element_type=F32)
        uv = jnp.dot(h, wv_ref[...], preferred_element_type=F32)
        cg = _causal_conv(ug, prev_g, cwg_ref[...], cbg_ref[...])
        cv = _causal_conv(uv, prev_v, cwv_ref[...], cbv_ref[...])
        prev_g, prev_v = ug[-8:], uv[-8:]
        act = (jax.nn.gelu(cg, approximate=True) * cv).astype(BF16)
        o_ref[rows, :] += jnp.dot(act, wd_ref[...], preferred_element_type=F32)
    carry_g[j] = prev_g
    carry_v[j] = prev_v

    @pl.when(j == pl.num_programs(1) - 1)
    def _():
        x_copy.wait()
        o_ref[...] = x_buf[...] + _rms(o_ref[...], g_ref[...])


def _ffn(h2, x1, w_up, conv_w, conv_b, w_down, g_post, tf, chunks):
    s_len, d = x1.shape
    d_ff = w_down.shape[0]
    nf = d_ff // tf
    tm = sum(chunks)
    assert d_ff % tf == 0 and s_len % tm == 0
    return pl.pallas_call(
        functools.partial(_ffn_kernel, chunks=chunks),
        grid=(s_len // tm, nf),
        in_specs=[
            pl.BlockSpec((tm, d), lambda i, j: (i, 0)),
            pl.BlockSpec((d, tf), lambda i, j: (0, j)),
            pl.BlockSpec((d, tf), lambda i, j: (0, nf + j)),
            pl.BlockSpec((CONV_WIDTH, tf), lambda i, j: (0, j)),
            pl.BlockSpec((CONV_WIDTH, tf), lambda i, j: (0, nf + j)),
            pl.BlockSpec((1, tf), lambda i, j: (0, j)),
            pl.BlockSpec((1, tf), lambda i, j: (0, nf + j)),
            pl.BlockSpec((tf, d), lambda i, j: (j, 0)),
            pl.BlockSpec(memory_space=pl.ANY),
            pl.BlockSpec((1, d), lambda i, j: (0, 0)),
        ],
        out_specs=pl.BlockSpec((tm, d), lambda i, j: (i, 0)),
        out_shape=jax.ShapeDtypeStruct((s_len, d), F32),
        scratch_shapes=[pltpu.VMEM((nf, 8, tf), F32), pltpu.VMEM((nf, 8, tf), F32),
                        pltpu.VMEM((tm, d), F32), pltpu.SemaphoreType.DMA(())],
        compiler_params=_params(2),
        name="conv_glu_ffn",
    )(h2, w_up, w_up, conv_w, conv_w, conv_b.reshape(1, -1), conv_b.reshape(1, -1), w_down,
      x1, g_post.reshape(1, d))


def _rel_bucket(dist):
    n = jnp.maximum(dist, 0)
    nf = jnp.maximum(n, 1).astype(F32)
    large = REL_MAX_EXACT + (jnp.log(nf / REL_MAX_EXACT)
                             / math.log(REL_MAX_DISTANCE / REL_MAX_EXACT)
                             * (REL_BUCKETS - REL_MAX_EXACT)).astype(jnp.int32)
    large = jnp.minimum(large, REL_BUCKETS - 1)
    return jnp.where(n < REL_MAX_EXACT, n, large)


def _bucket_blocks(dilation, max_dist, streams=1):
    per = BLOCK // streams
    idx = jnp.arange(BLOCK)
    member = streams * (idx % per) + idx // per
    key = jnp.concatenate([member, BLOCK + member])
    rel = member[:, None] + BLOCK - key[None, :]
    valid = (rel >= 0) & (rel <= max_dist)
    normal = jnp.where(valid, _rel_bucket(rel * dilation), -1)
    first = jnp.where((jnp.arange(2 * BLOCK) >= BLOCK)[None, :], normal, -1)
    return jnp.stack([first, normal]).astype(jnp.int32)


def kernel(x, mem, rel_bias, g_mix_pre, w_in, a_sinks, g_mem, w_mem_kv, g_out, w_out, g_mix_post,
           g_ffn_pre, w_up, conv_w, conv_b, w_down, g_ffn_post):
    bn, s_len, d = x.shape
    assert bn == 1
    depth = w_in.shape[0]
    xs = x.reshape(s_len, d)
    mems = mem.reshape(mem.shape[1], d)

    buckets_a = _bucket_blocks(1, A_WINDOW - 1)
    buckets_b = [_bucket_blocks(dil, window // dil, _streams(dil)) for window, dil in B_PATTERNS]

    off_ka, off_va, off_qb = A_Q, A_Q + A_KV, A_Q + 2 * A_KV
    off_kb, off_vb, off_qm = off_qb + B_W, off_qb + 2 * B_W, off_qb + 3 * B_W
    proj_outs = (((0,), A_Q, Q_SCALE), ((off_ka,), A_KV, 1.0), ((off_va,), A_KV, 1.0),
                 ((off_qb, off_qb + B_QW), B_QW, Q_SCALE), ((off_kb, off_vb), B_W, 1.0),
                 ((off_qm,), M_W, Q_SCALE))
    mem_outs = (((0,), M_W, 1.0), ((M_W,), M_W, 1.0))

    w_in_l = w_in[0].astype(BF16)
    tq = 2048

    for l in range(depth):
        cast = [(w_up, l), (w_down, l), (w_out, l)] + ([(w_in, l + 1)] if l + 1 < depth else [])
        qa, ka, va, qbw, kvbw, qm, w_up_l, w_down_l, w_out_l, *w_in_next = _norm_proj(
            xs, g_mix_pre[l], w_in_l, proj_outs, tm=512, cast=cast)
        w_in_l = w_in_next[0] if w_in_next else None
        mk, mv = _norm_proj(mems, g_mem[l], w_mem_kv[l].astype(BF16), mem_outs, tm=mems.shape[0])
        ya, ym = _attn_am(qa, ka, va, qm, mk, mv, a_sinks[l], rel_bias, buckets_a,
                          g_out[l, :A_Q], g_out[l, A_Q + B_W:], tq=tq)
        obs, lses = [], []
        for (window, dil), buckets in zip(B_PATTERNS, buckets_b):
            if dil == 1:
                o, lse = _attn_b_nat(qbw, kvbw, rel_bias, buckets, tq=tq)
            else:
                o, lse = _attn_b_dil(qbw, kvbw, rel_bias, buckets, dil, tq=min(tq, s_len // dil))
            obs.append(o)
            lses.append(lse)
        x1, h2 = _out_proj(xs, ya, ym, obs, lses, g_out[l, A_Q:A_Q + B_W], w_out_l,
                           g_mix_post[l], g_ffn_pre[l], tm=512, chunk=128)
        xs = _ffn(h2, x1, w_up_l, conv_w[l], conv_b[l], w_down_l, g_ffn_post[l],
                  tf=512, chunks=((512, 512), (640, 384))[l])
    return xs.reshape(bn, s_len, d)
```

```python
import functools
import math

import jax
import jax.numpy as jnp
from jax import lax
from jax.experimental import pallas as pl
from jax.experimental.pallas import tpu as pltpu

F32 = jnp.float32
BF16 = jnp.bfloat16
U32 = jnp.uint32

HEAD_DIM = 128
BLOCK = 128
A_HEADS, A_KV_HEADS, A_GROUP = 6, 2, 3
A_WINDOW = 128
B_HEADS = 6
B_PATTERNS = ((128, 1), (512, 4), (2048, 16))
M_HEADS = 4
REL_BUCKETS, REL_MAX_EXACT, REL_MAX_DISTANCE = 32, 16, 2048
CONV_WIDTH = 3
EPS = 1e-6
NEG = -1e30

A_Q = A_HEADS * HEAD_DIM
A_KV = A_KV_HEADS * HEAD_DIM
B_W = B_HEADS * HEAD_DIM
M_W = M_HEADS * HEAD_DIM
B_QW = B_W // 2
LOG2E = math.log2(math.e)
Q_SCALE = HEAD_DIM ** -0.5 * LOG2E

VMEM_LIMIT_BYTES = 56 * 1024 * 1024
SUBLANES = 8


def _params(n_axes):
    return pltpu.CompilerParams(dimension_semantics=("arbitrary",) * n_axes,
                                vmem_limit_bytes=VMEM_LIMIT_BYTES)


def _rms(x, g):
    return x * lax.rsqrt(jnp.mean(x * x, axis=-1, keepdims=True) + EPS) * g


def _dot_nt(a, b):
    return lax.dot_general(a, b, (((1,), (1,)), ((), ())), preferred_element_type=F32)


def _softmax_parts(s, sink=None):
    m = jnp.max(s, axis=-1, keepdims=True)
    if sink is not None:
        m = jnp.maximum(m, sink)
    p = jnp.exp2(s - m)
    den = jnp.sum(p, axis=-1, keepdims=True)
    if sink is not None:
        den = den + jnp.exp2(sink - m)
    return p, m, den


def _pack_pair(hi, lo):
    hi_bits = lax.bitcast_convert_type(hi.astype(BF16).astype(F32), U32)
    lo_bits = lax.bitcast_convert_type(lo.astype(BF16).astype(F32), U32)
    return hi_bits | (lo_bits >> 16)


def _unpack_hi(words):
    return lax.bitcast_convert_type(words & jnp.uint32(0xFFFF0000), F32).astype(BF16)


def _unpack_lo(words):
    return lax.bitcast_convert_type(words << 16, F32).astype(BF16)


def _proj_kernel(x_ref, g_ref, w_ref, *refs, outs, n_cast):
    cast_in, o_refs = refs[:n_cast], refs[n_cast:n_cast + len(outs)]
    cast_out = refs[n_cast + len(outs):]
    h = _rms(x_ref[...], g_ref[...]).astype(BF16)

    def cols(off, width, scale):
        acc = jnp.dot(h, w_ref[:, off:off + width], preferred_element_type=F32)
        return acc if scale == 1.0 else acc * scale

    for o_ref, (offs, width, scale) in zip(o_refs, outs):
        if len(offs) == 1:
            o_ref[...] = cols(offs[0], width, scale).astype(BF16)
        else:
            o_ref[...] = _pack_pair(cols(offs[0], width, scale), cols(offs[1], width, scale))

    for src, dst in zip(cast_in, cast_out):
        dst[...] = src[...].astype(BF16)


def _norm_proj(x, g, w, outs, tm, cast=()):
    rows, d = x.shape
    n = w.shape[1]
    steps = rows // tm
    assert rows % tm == 0 and all(a.shape[1] % (16 * steps) == 0 for a, _ in cast)
    return pl.pallas_call(
        functools.partial(_proj_kernel, outs=outs, n_cast=len(cast)),
        grid=(steps,),
        in_specs=[
            pl.BlockSpec((tm, d), lambda i: (i, 0)),
            pl.BlockSpec((1, d), lambda i: (0, 0)),
            pl.BlockSpec((d, n), lambda i: (0, 0), pipeline_mode=pl.Buffered(1)),
        ] + [pl.BlockSpec((None, a.shape[1] // steps, a.shape[2]),
                          functools.partial(lambda layer, i: (layer, i, 0), layer))
             for a, layer in cast],
        out_specs=[pl.BlockSpec((tm, width), lambda i: (i, 0)) for _, width, _ in outs]
        + [pl.BlockSpec((a.shape[1] // steps, a.shape[2]), lambda i: (i, 0)) for a, _ in cast],
        out_shape=[jax.ShapeDtypeStruct((rows, width), BF16 if len(offs) == 1 else U32)
                   for offs, width, _ in outs]
        + [jax.ShapeDtypeStruct(a.shape[1:], BF16) for a, _ in cast],
        compiler_params=_params(1),
        name="norm_proj",
    )(x, g.reshape(1, d), w, *(a for a, _ in cast))


def _lookup_bias(table_ref, bucket, head):
    def body(b, acc):
        return jnp.where(bucket == b, table_ref[b, head] * LOG2E, acc)
    return lax.fori_loop(0, REL_BUCKETS, body, jnp.full(bucket.shape, NEG, F32))


def _attn_am_kernel(sink_ref, table_ref, bucket_ref, qa_ref, ka_ref, kap_ref, va_ref, vap_ref,
                    qm_ref, mk_ref, mv_ref, ga_ref, gm_ref, ya_ref, ym_ref, bias_scr, *, nsub):
    i = pl.program_id(0)

    @pl.when(i == 0)
    def _():
        for var in range(2):
            for h in range(A_HEADS):
                kv, g = divmod(h, A_GROUP)
                bias_scr[var, kv, g * BLOCK:(g + 1) * BLOCK, :] = _lookup_bias(
                    table_ref, bucket_ref[var], h)

    for sb in range(nsub):
        rows = slice(sb * BLOCK, (sb + 1) * BLOCK)
        prev_rows = slice((sb - 1) * BLOCK, sb * BLOCK)
        b_ref = bias_scr.at[jnp.minimum(i, 1) if sb == 0 else 1]
        outs = []
        for kv in range(A_KV_HEADS):
            cols = slice(kv * HEAD_DIM, (kv + 1) * HEAD_DIM)
            if sb == 0:
                k_prev, v_prev = kap_ref[:, cols], vap_ref[:, cols]
            else:
                k_prev, v_prev = ka_ref[prev_rows, cols], va_ref[prev_rows, cols]
            kk = jnp.concatenate([k_prev, ka_ref[rows, cols]], axis=0)
            vv = jnp.concatenate([v_prev, va_ref[rows, cols]], axis=0)
            heads = [kv * A_GROUP + g for g in range(A_GROUP)]
            q3 = jnp.concatenate(
                [qa_ref[rows, h * HEAD_DIM:(h + 1) * HEAD_DIM] for h in heads], axis=0)
            s = _dot_nt(q3, kk) + b_ref[kv]
            sink = jnp.concatenate(
                [jnp.full((BLOCK, 1), sink_ref[h] * LOG2E, F32) for h in heads], axis=0)
            p, _, den = _softmax_parts(s, sink)
            o = jnp.dot(p.astype(BF16), vv, preferred_element_type=F32) / den
            outs += [o[g * BLOCK:(g + 1) * BLOCK] for g in range(A_GROUP)]
        oa = jnp.concatenate(outs, axis=1)
        ya_ref[rows, :] = _rms(oa, ga_ref[...]).astype(BF16)

    outs = []
    for h in range(M_HEADS):
        cols = slice(h * HEAD_DIM, (h + 1) * HEAD_DIM)
        s = _dot_nt(qm_ref[:, cols], mk_ref[:, cols])
        p, _, den = _softmax_parts(s)
        outs.append(jnp.dot(p.astype(BF16), mv_ref[:, cols], preferred_element_type=F32) / den)
    ym_ref[...] = _rms(jnp.concatenate(outs, axis=1), gm_ref[...]).astype(BF16)


def _attn_am(qa, ka, va, qm, mk, mv, sinks, table, buckets, g_a, g_m, tq):
    s_len = qa.shape[0]
    nsub = tq // BLOCK
    mem_len = mk.shape[0]
    prev = lambda i: (jnp.maximum(i * nsub - 1, 0), 0)
    return pl.pallas_call(
        functools.partial(_attn_am_kernel, nsub=nsub),
        grid=(s_len // tq,),
        in_specs=[
            pl.BlockSpec(memory_space=pltpu.SMEM),
            pl.BlockSpec(memory_space=pltpu.SMEM),
            pl.BlockSpec(buckets.shape, lambda i: (0, 0, 0)),
            pl.BlockSpec((tq, A_Q), lambda i: (i, 0)),
            pl.BlockSpec((tq, A_KV), lambda i: (i, 0)),
            pl.BlockSpec((BLOCK, A_KV), prev),
            pl.BlockSpec((tq, A_KV), lambda i: (i, 0)),
            pl.BlockSpec((BLOCK, A_KV), prev),
            pl.BlockSpec((tq, M_W), lambda i: (i, 0)),
            pl.BlockSpec((mem_len, M_W), lambda i: (0, 0)),
            pl.BlockSpec((mem_len, M_W), lambda i: (0, 0)),
            pl.BlockSpec((1, A_Q), lambda i: (0, 0)),
            pl.BlockSpec((1, M_W), lambda i: (0, 0)),
        ],
        out_specs=[pl.BlockSpec((tq, A_Q), lambda i: (i, 0)),
                   pl.BlockSpec((tq, M_W), lambda i: (i, 0))],
        out_shape=[jax.ShapeDtypeStruct((s_len, A_Q), BF16),
                   jax.ShapeDtypeStruct((s_len, M_W), BF16)],
        scratch_shapes=[pltpu.VMEM((2, A_KV_HEADS, A_GROUP * BLOCK, 2 * BLOCK), F32)],
        compiler_params=_params(1),
        name="attn_am",
    )(sinks, table, buckets, qa, ka, ka, va, va, qm, mk, mv,
      g_a.reshape(1, A_Q), g_m.reshape(1, M_W))


def _build_bias_b(table_ref, bucket_ref, bias_scr):
    for var in range(2):
        for h in range(B_HEADS):
            bias_scr[var, h] = _lookup_bias(table_ref, bucket_ref[var], A_HEADS + h)


def _attn_b_tiles(qw_ref, kvw_ref, kvw_prev_ref, bias_scr, first_var, o_ref, lse_ref, nsub):
    ns = qw_ref.shape[0]
    per = BLOCK // ns
    half = B_HEADS // 2
    lane = lax.broadcasted_iota(jnp.int32, (BLOCK, HEAD_DIM), 1)

    def block(ref, sb, cols):
        return jnp.concatenate([ref[st, sb * per:(sb + 1) * per, cols] for st in range(ns)], axis=0)

    unpacked = {}
    for sb in range(nsub):
        b_ref = bias_scr.at[first_var if sb == 0 else 1]
        lse_tile = jnp.zeros((BLOCK, HEAD_DIM), F32)
        for h in range(B_HEADS):
            cols = slice(h * HEAD_DIM, (h + 1) * HEAD_DIM)
            qw = block(qw_ref, sb, slice((h % half) * HEAD_DIM, (h % half + 1) * HEAD_DIM))
            q = _unpack_hi(qw) if h < half else _unpack_lo(qw)
            if sb == 0:
                w_prev = block(kvw_prev_ref, 0, cols)
                k_prev, v_prev = _unpack_hi(w_prev), _unpack_lo(w_prev)
            else:
                k_prev, v_prev = unpacked.pop((sb - 1, h))
            w_cur = block(kvw_ref, sb, cols)
            k_cur, v_cur = _unpack_hi(w_cur), _unpack_lo(w_cur)
            unpacked[sb, h] = (k_cur, v_cur)
            kk = jnp.concatenate([k_prev, k_cur], axis=0)
            vv = jnp.concatenate([v_prev, v_cur], axis=0)
            s = _dot_nt(q, kk) + b_ref[h]
            p, m, den = _softmax_parts(s)
            o = jnp.dot(p.astype(BF16), vv, preferred_element_type=F32) / den
            for st in range(ns):
                o_ref[st, sb * per:(sb + 1) * per, cols] = o[st * per:(st + 1) * per]
            lse_tile = jnp.where(lane == h, m + jnp.log2(den), lse_tile)
        for st in range(ns):
            lse_ref[st, sb * per:(sb + 1) * per, :] = lse_tile[st * per:(st + 1) * per]


def _attn_b_nat_kernel(table_ref, bucket_ref, qw_ref, kvw_ref, kvwp_ref, o_ref, lse_ref, bias_scr,
                       *, nsub):
    i = pl.program_id(0)

    @pl.when(i == 0)
    def _():
        _build_bias_b(table_ref, bucket_ref, bias_scr)

    _attn_b_tiles(qw_ref, kvw_ref, kvwp_ref, bias_scr, jnp.minimum(i, 1), o_ref, lse_ref, nsub)


def _attn_b_dil_kernel(table_ref, bucket_ref, qw_hbm, kvw_hbm, o_hbm, lse_hbm,
                       qbuf, kvbuf, obuf, lbuf, carry, bias_scr, in_sems, out_sems,
                       *, nsub, steps_per_class, dil):
    s, n_steps = pl.program_id(0), pl.num_programs(0)
    ns = qbuf.shape[1]
    rows_per = qbuf.shape[2]
    slot = lax.rem(s, 2)

    def class_rows(step):
        c, i = lax.div(step, steps_per_class), lax.rem(step, steps_per_class)
        return [(pl.ds(i * rows_per, rows_per), c + st * dil) for st in range(ns)]

    def in_copies(step, sl):
        return [pltpu.make_async_copy(hbm.at[rows, col, :], buf.at[sl, st], in_sems.at[a, st, sl])
                for a, (hbm, buf) in enumerate(((qw_hbm, qbuf), (kvw_hbm, kvbuf)))
                for st, (rows, col) in enumerate(class_rows(step))]

    def out_copies(step, sl):
        return [pltpu.make_async_copy(buf.at[sl, st], hbm.at[rows, col, :], out_sems.at[a, st, sl])
                for a, (hbm, buf) in enumerate(((o_hbm, obuf), (lse_hbm, lbuf)))
                for st, (rows, col) in enumerate(class_rows(step))]

    @pl.when(s == 0)
    def _():
        for cp in in_copies(0, 0):
            cp.start()
        _build_bias_b(table_ref, bucket_ref, bias_scr)
        carry[...] = jnp.zeros(carry.shape, U32)

    @pl.when(s + 1 < n_steps)
    def _():
        for cp in in_copies(s + 1, 1 - slot):
            cp.start()

    for cp in in_copies(s, slot):
        cp.wait()

    @pl.when(s >= 2)
    def _():
        for cp in out_copies(s - 2, slot):
            cp.wait()

    first_var = jnp.minimum(lax.rem(s, steps_per_class), 1)
    _attn_b_tiles(qbuf.at[slot], kvbuf.at[slot], carry, bias_scr, first_var,
                  obuf.at[slot], lbuf.at[slot], nsub)
    carry[...] = kvbuf[slot, :, rows_per - carry.shape[1]:, :]

    for cp in out_copies(s, slot):
        cp.start()

    @pl.when(s == n_steps - 1)
    def _():
        for cp in out_copies(s - 1, 1 - slot) + out_copies(s, slot):
            cp.wait()


def _attn_b_nat(qw, kvw, table, buckets, tq):
    s_len = qw.shape[0]
    nsub = tq // BLOCK
    cur = lambda i: (0, i, 0)
    prev = lambda i: (0, jnp.maximum(i * nsub - 1, 0), 0)
    o, lse = pl.pallas_call(
        functools.partial(_attn_b_nat_kernel, nsub=nsub),
        grid=(s_len // tq,),
        in_specs=[
            pl.BlockSpec(memory_space=pltpu.SMEM),
            pl.BlockSpec(buckets.shape, lambda i: (0, 0, 0)),
            pl.BlockSpec((1, tq, B_QW), cur),
            pl.BlockSpec((1, tq, B_W), cur),
            pl.BlockSpec((1, BLOCK, B_W), prev),
        ],
        out_specs=[pl.BlockSpec((1, tq, B_W), cur), pl.BlockSpec((1, tq, HEAD_DIM), cur)],
        out_shape=[jax.ShapeDtypeStruct((1, s_len, B_W), F32),
                   jax.ShapeDtypeStruct((1, s_len, HEAD_DIM), F32)],
        scratch_shapes=[pltpu.VMEM((2, B_HEADS, BLOCK, 2 * BLOCK), F32)],
        compiler_params=_params(1),
        name="attn_b_d1",
    )(table, buckets, qw[None], kvw[None], kvw[None])
    return o[0], lse[0]


def _streams(dil):
    return 1 if dil == 1 else max(1, 8 // dil)


def _attn_b_dil(qw, kvw, table, buckets, dil, tq):
    s_len = qw.shape[0]
    ns = _streams(dil)
    n = dil * ns
    rows_per = tq // ns
    steps_per_class = s_len // dil // tq
    view = lambda t: t.reshape(s_len // n, n, t.shape[-1])
    any_spec = pl.BlockSpec(memory_space=pl.ANY)
    o, lse = pl.pallas_call(
        functools.partial(_attn_b_dil_kernel, nsub=tq // BLOCK, steps_per_class=steps_per_class,
                          dil=dil),
        grid=(dil * steps_per_class,),
        in_specs=[pl.BlockSpec(memory_space=pltpu.SMEM),
                  pl.BlockSpec(buckets.shape, lambda s: (0, 0, 0)), any_spec, any_spec],
        out_specs=[any_spec, any_spec],
        out_shape=[jax.ShapeDtypeStruct((s_len // n, n, B_W), F32),
                   jax.ShapeDtypeStruct((s_len // n, n, HEAD_DIM), F32)],
        scratch_shapes=[
            pltpu.VMEM((2, ns, rows_per, B_QW), U32),
            pltpu.VMEM((2, ns, rows_per, B_W), U32),
            pltpu.VMEM((2, ns, rows_per, B_W), F32),
            pltpu.VMEM((2, ns, rows_per, HEAD_DIM), F32),
            pltpu.VMEM((ns, BLOCK // ns, B_W), U32),
            pltpu.VMEM((2, B_HEADS, BLOCK, 2 * BLOCK), F32),
            pltpu.SemaphoreType.DMA((2, ns, 2)),
            pltpu.SemaphoreType.DMA((2, ns, 2)),
        ],
        compiler_params=_params(1),
        name=f"attn_b_d{dil}",
    )(table, buckets, view(qw), view(kvw))
    return o.reshape(s_len, B_W), lse.reshape(s_len, HEAD_DIM)


def _out_kernel(x_ref, ya_ref, ym_ref, o1_ref, o2_ref, o3_ref, l1_ref, l2_ref, l3_ref,
                gb_ref, w_ref, gpost_ref, gpre_ref, x1_ref, h2_ref, *, chunk):
    o_refs = (o1_ref, o2_ref, o3_ref)
    for c in range(x_ref.shape[0] // chunk):
        rows = slice(c * chunk, (c + 1) * chunk)
        obs = []
        for h in range(B_HEADS):
            cols = slice(h * HEAD_DIM, (h + 1) * HEAD_DIM)
            ls = [l_ref[rows, h:h + 1] for l_ref in (l1_ref, l2_ref, l3_ref)]
            mx = jnp.maximum(jnp.maximum(ls[0], ls[1]), ls[2])
            es = [jnp.exp2(l - mx) for l in ls]
            inv = 1.0 / (es[0] + es[1] + es[2])
            obs.append(sum((e * inv) * o_ref[rows, cols] for e, o_ref in zip(es, o_refs)))
        yb = _rms(jnp.concatenate(obs, axis=1), gb_ref[...]).astype(BF16)
        y_in = jnp.concatenate([ya_ref[rows, :], yb, ym_ref[rows, :]], axis=1)
        y = jnp.dot(y_in, w_ref[...], preferred_element_type=F32)
        x1 = x_ref[rows, :] + _rms(y, gpost_ref[...])
        x1_ref[rows, :] = x1
        h2_ref[rows, :] = _rms(x1, gpre_ref[...]).astype(BF16)


def _out_proj(x, ya, ym, obs, lses, g_b, w_out, g_post, g_pre, tm, chunk):
    s_len, d = x.shape
    row = lambda width: pl.BlockSpec((tm, width), lambda i: (i, 0))
    vec = lambda width: pl.BlockSpec((1, width), lambda i: (0, 0))
    return pl.pallas_call(
        functools.partial(_out_kernel, chunk=chunk),
        grid=(s_len // tm,),
        in_specs=[row(d), row(A_Q), row(M_W), row(B_W), row(B_W), row(B_W),
                  row(HEAD_DIM), row(HEAD_DIM), row(HEAD_DIM), vec(B_W),
                  pl.BlockSpec(w_out.shape, lambda i: (0, 0), pipeline_mode=pl.Buffered(1)),
                  vec(d), vec(d)],
        out_specs=[row(d), row(d)],
        out_shape=[jax.ShapeDtypeStruct((s_len, d), F32), jax.ShapeDtypeStruct((s_len, d), BF16)],
        compiler_params=_params(1),
        name="out_proj",
    )(x, ya, ym, *obs, *lses, g_b.reshape(1, B_W), w_out, g_post.reshape(1, d),
      g_pre.reshape(1, d))


def _shift_rows(u, carry, shift):
    rows, n = u.shape
    above = jnp.concatenate([carry, u[:-SUBLANES]], axis=0).reshape(rows // SUBLANES, SUBLANES, n)
    from_above = lax.broadcasted_iota(jnp.int32, (1, SUBLANES, n), 1) >= SUBLANES - shift
    mixed = jnp.where(from_above, above, u.reshape(rows // SUBLANES, SUBLANES, n))
    return pltpu.roll(mixed, shift, axis=1).reshape(rows, n)


def _causal_conv(u, carry, cw, cb):
    return (cw[0:1] * _shift_rows(u, carry, 2) + cw[1:2] * _shift_rows(u, carry, 1)
            + cw[2:3] * u + cb)


def _ffn_kernel(h_ref, wg_ref, wv_ref, cwg_ref, cwv_ref, cbg_ref, cbv_ref, wd_ref, x_hbm, g_ref,
                o_ref, carry_g, carry_v, x_buf, x_sem, *, chunks):
    i, j = pl.program_id(0), pl.program_id(1)
    tm = x_buf.shape[0]
    x_copy = pltpu.make_async_copy(x_hbm.at[pl.ds(i * tm, tm), :], x_buf, x_sem)

    @pl.when(i == 0)
    def _():
        carry_g[j] = jnp.zeros(carry_g.shape[1:], F32)
        carry_v[j] = jnp.zeros(carry_v.shape[1:], F32)

    @pl.when(j == 0)
    def _():
        x_copy.start()
        o_ref[...] = jnp.zeros(o_ref.shape, F32)

    prev_g, prev_v = carry_g[j], carry_v[j]
    starts = [sum(chunks[:c]) for c in range(len(chunks))]
    for start, size in zip(starts, chunks):
        rows = slice(start, start + size)
        h = h_ref[rows, :]
        ug = jnp.dot(h, wg_ref[...], preferred_element_type=F32)
        uv = jnp.dot(h, wv_ref[...], preferred_element_type=F32)
        cg = _causal_conv(ug, prev_g, cwg_ref[...], cbg_ref[...])
        cv = _causal_conv(uv, prev_v, cwv_ref[...], cbv_ref[...])
        prev_g, prev_v = ug[-8:], uv[-8:]
        act = (jax.nn.gelu(cg, approximate=True) * cv).astype(BF16)
        o_ref[rows, :] += jnp.dot(act, wd_ref[...], preferred_element_type=F32)
    carry_g[j] = prev_g
    carry_v[j] = prev_v

    @pl.when(j == pl.num_programs(1) - 1)
    def _():
        x_copy.wait()
        o_ref[...] = x_buf[...] + _rms(o_ref[...], g_ref[...])


def _ffn(h2, x1, w_up, conv_w, conv_b, w_down, g_post, tf, chunks):
    s_len, d = x1.shape
    d_ff = w_down.shape[0]
    nf = d_ff // tf
    tm = sum(chunks)
    assert d_ff % tf == 0 and s_len % tm == 0
    return pl.pallas_call(
        functools.partial(_ffn_kernel, chunks=chunks),
        grid=(s_len // tm, nf),
        in_specs=[
            pl.BlockSpec((tm, d), lambda i, j: (i, 0)),
            pl.BlockSpec((d, tf), lambda i, j: (0, j)),
            pl.BlockSpec((d, tf), lambda i, j: (0, nf + j)),
            pl.BlockSpec((CONV_WIDTH, tf), lambda i, j: (0, j)),
            pl.BlockSpec((CONV_WIDTH, tf), lambda i, j: (0, nf + j)),
            pl.BlockSpec((1, tf), lambda i, j: (0, j)),
            pl.BlockSpec((1, tf), lambda i, j: (0, nf + j)),
            pl.BlockSpec((tf, d), lambda i, j: (j, 0)),
            pl.BlockSpec(memory_space=pl.ANY),
            pl.BlockSpec((1, d), lambda i, j: (0, 0)),
        ],
        out_specs=pl.BlockSpec((tm, d), lambda i, j: (i, 0)),
        out_shape=jax.ShapeDtypeStruct((s_len, d), F32),
        scratch_shapes=[pltpu.VMEM((nf, 8, tf), F32), pltpu.VMEM((nf, 8, tf), F32),
                        pltpu.VMEM((tm, d), F32), pltpu.SemaphoreType.DMA(())],
        compiler_params=_params(2),
        name="conv_glu_ffn",
    )(h2, w_up, w_up, conv_w, conv_w, conv_b.reshape(1, -1), conv_b.reshape(1, -1), w_down,
      x1, g_post.reshape(1, d))


def _rel_bucket(dist):
    n = jnp.maximum(dist, 0)
    nf = jnp.maximum(n, 1).astype(F32)
    large = REL_MAX_EXACT + (jnp.log(nf / REL_MAX_EXACT)
                             / math.log(REL_MAX_DISTANCE / REL_MAX_EXACT)
                             * (REL_BUCKETS - REL_MAX_EXACT)).astype(jnp.int32)
    large = jnp.minimum(large, REL_BUCKETS - 1)
    return jnp.where(n < REL_MAX_EXACT, n, large)


def _bucket_blocks(dilation, max_dist, streams=1):
    per = BLOCK // streams
    idx = jnp.arange(BLOCK)
    member = streams * (idx % per) + idx // per
    key = jnp.concatenate([member, BLOCK + member])
    rel = member[:, None] + BLOCK - key[None, :]
    valid = (rel >= 0) & (rel <= max_dist)
    normal = jnp.where(valid, _rel_bucket(rel * dilation), -1)
    first = jnp.where((jnp.arange(2 * BLOCK) >= BLOCK)[None, :], normal, -1)
    return jnp.stack([first, normal]).astype(jnp.int32)


def kernel(x, mem, rel_bias, g_mix_pre, w_in, a_sinks, g_mem, w_mem_kv, g_out, w_out, g_mix_post,
           g_ffn_pre, w_up, conv_w, conv_b, w_down, g_ffn_post):
    bn, s_len, d = x.shape
    assert bn == 1
    depth = w_in.shape[0]
    xs = x.reshape(s_len, d)
    mems = mem.reshape(mem.shape[1], d)

    buckets_a = _bucket_blocks(1, A_WINDOW - 1)
    buckets_b = [_bucket_blocks(dil, window // dil, _streams(dil)) for window, dil in B_PATTERNS]

    off_ka, off_va, off_qb = A_Q, A_Q + A_KV, A_Q + 2 * A_KV
    off_kb, off_vb, off_qm = off_qb + B_W, off_qb + 2 * B_W, off_qb + 3 * B_W
    proj_outs = (((0,), A_Q, Q_SCALE), ((off_ka,), A_KV, 1.0), ((off_va,), A_KV, 1.0),
                 ((off_qb, off_qb + B_QW), B_QW, Q_SCALE), ((off_kb, off_vb), B_W, 1.0),
                 ((off_qm,), M_W, Q_SCALE))
    mem_outs = (((0,), M_W, 1.0), ((M_W,), M_W, 1.0))

    w_in_l = w_in[0].astype(BF16)
    tq = 2048

    for l in range(depth):
        cast = [(w_up, l), (w_down, l), (w_out, l)] + ([(w_in, l + 1)] if l + 1 < depth else [])
        qa, ka, va, qbw, kvbw, qm, w_up_l, w_down_l, w_out_l, *w_in_next = _norm_proj(
            xs, g_mix_pre[l], w_in_l, proj_outs, tm=512, cast=cast)
        w_in_l = w_in_next[0] if w_in_next else None
        mk, mv = _norm_proj(mems, g_mem[l], w_mem_kv[l].astype(BF16), mem_outs, tm=mems.shape[0])
        ya, ym = _attn_am(qa, ka, va, qm, mk, mv, a_sinks[l], rel_bias, buckets_a,
                          g_out[l, :A_Q], g_out[l, A_Q + B_W:], tq=tq)
        obs, lses = [], []
        for (window, dil), buckets in zip(B_PATTERNS, buckets_b):
            if dil == 1:
                o, lse = _attn_b_nat(qbw, kvbw, rel_bias, buckets, tq=tq)
            else:
                o, lse = _attn_b_dil(qbw, kvbw, rel_bias, buckets, dil, tq=min(tq, s_len // dil))
            obs.append(o)
            lses.append(lse)
        x1, h2 = _out_proj(xs, ya, ym, obs, lses, g_out[l, A_Q:A_Q + B_W], w_out_l,
                           g_mix_post[l], g_ffn_pre[l], tm=512, chunk=128)
        xs = _ffn(h2, x1, w_up_l, conv_w[l], conv_b[l], w_down_l, g_ffn_post[l],
                  tf=512, chunks=(640, 384))
    return xs.reshape(bn, s_len, d)
```

```python
import functools
import math

import jax
import jax.numpy as jnp
from jax import lax
from jax.experimental import pallas as pl
from jax.experimental.pallas import tpu as pltpu

F32 = jnp.float32
BF16 = jnp.bfloat16
U32 = jnp.uint32

HEAD_DIM = 128
BLOCK = 128
A_HEADS, A_KV_HEADS, A_GROUP = 6, 2, 3
A_WINDOW = 128
B_HEADS = 6
B_PATTERNS = ((128, 1), (512, 4), (2048, 16))
M_HEADS = 4
REL_BUCKETS, REL_MAX_EXACT, REL_MAX_DISTANCE = 32, 16, 2048
CONV_WIDTH = 3
EPS = 1e-6
NEG = -1e30

A_Q = A_HEADS * HEAD_DIM
A_KV = A_KV_HEADS * HEAD_DIM
B_W = B_HEADS * HEAD_DIM
M_W = M_HEADS * HEAD_DIM
B_QW = B_W // 2
LOG2E = math.log2(math.e)
Q_SCALE = HEAD_DIM ** -0.5 * LOG2E

VMEM_LIMIT_BYTES = 56 * 1024 * 1024
SUBLANES = 8
BF16_ROWS = 16


def _params(n_axes):
    return pltpu.CompilerParams(dimension_semantics=("arbitrary",) * n_axes,
                                vmem_limit_bytes=VMEM_LIMIT_BYTES)


def _rms(x, g):
    return x * lax.rsqrt(jnp.mean(x * x, axis=-1, keepdims=True) + EPS) * g


def _dot_nt(a, b):
    return lax.dot_general(a, b, (((1,), (1,)), ((), ())), preferred_element_type=F32)


def _softmax_parts(s, sink=None):
    m = jnp.max(s, axis=-1, keepdims=True)
    if sink is not None:
        m = jnp.maximum(m, sink)
    p = jnp.exp2(s - m)
    den = jnp.sum(p, axis=-1, keepdims=True)
    if sink is not None:
        den = den + jnp.exp2(sink - m)
    return p, m, den


def _pack_pair(hi, lo):
    hi_bits = lax.bitcast_convert_type(hi.astype(BF16).astype(F32), U32)
    lo_bits = lax.bitcast_convert_type(lo.astype(BF16).astype(F32), U32)
    return hi_bits | (lo_bits >> 16)


def _unpack_hi(words):
    return lax.bitcast_convert_type(words & jnp.uint32(0xFFFF0000), F32).astype(BF16)


def _unpack_lo(words):
    return lax.bitcast_convert_type(words << 16, F32).astype(BF16)


def _proj_kernel(x_ref, g_ref, w_ref, *refs, outs, n_cast):
    cast_in, o_refs = refs[:n_cast], refs[n_cast:n_cast + len(outs)]
    cast_out = refs[n_cast + len(outs):]
    h = _rms(x_ref[...], g_ref[...]).astype(BF16)

    def cols(off, width, scale):
        acc = jnp.dot(h, w_ref[:, off:off + width], preferred_element_type=F32)
        return acc if scale == 1.0 else acc * scale

    for o_ref, (offs, width, scale) in zip(o_refs, outs):
        if len(offs) == 1:
            o_ref[...] = cols(offs[0], width, scale).astype(BF16)
        else:
            o_ref[...] = _pack_pair(cols(offs[0], width, scale), cols(offs[1], width, scale))

    for src, dst in zip(cast_in, cast_out):
        dst[...] = src[...].astype(BF16)


def _norm_proj(x, g, w, outs, tm, cast=()):
    rows, d = x.shape
    n = w.shape[1]
    steps = rows // tm
    assert rows % tm == 0 and all(a.shape[1] % (BF16_ROWS * steps) == 0 for a, _ in cast)
    return pl.pallas_call(
        functools.partial(_proj_kernel, outs=outs, n_cast=len(cast)),
        grid=(steps,),
        in_specs=[
            pl.BlockSpec((tm, d), lambda i: (i, 0)),
            pl.BlockSpec((1, d), lambda i: (0, 0)),
            pl.BlockSpec((d, n), lambda i: (0, 0), pipeline_mode=pl.Buffered(1)),
        ] + [pl.BlockSpec((None, a.shape[1] // steps, a.shape[2]),
                          functools.partial(lambda layer, i: (layer, i, 0), layer))
             for a, layer in cast],
        out_specs=[pl.BlockSpec((tm, width), lambda i: (i, 0)) for _, width, _ in outs]
        + [pl.BlockSpec((a.shape[1] // steps, a.shape[2]), lambda i: (i, 0)) for a, _ in cast],
        out_shape=[jax.ShapeDtypeStruct((rows, width), BF16 if len(offs) == 1 else U32)
                   for offs, width, _ in outs]
        + [jax.ShapeDtypeStruct(a.shape[1:], BF16) for a, _ in cast],
        compiler_params=_params(1),
        name="norm_proj",
    )(x, g.reshape(1, d), w, *(a for a, _ in cast))


def _lookup_bias(table_ref, bucket, head):
    def body(b, acc):
        return jnp.where(bucket == b, table_ref[b, head] * LOG2E, acc)
    return lax.fori_loop(0, REL_BUCKETS, body, jnp.full(bucket.shape, NEG, F32))


def _attn_am_kernel(sink_ref, table_ref, bucket_ref, qa_ref, ka_ref, kap_ref, va_ref, vap_ref,
                    qm_ref, mk_ref, mv_ref, ga_ref, gm_ref, ya_ref, ym_ref, bias_scr, *, nsub):
    i = pl.program_id(0)

    @pl.when(i == 0)
    def _():
        for var in range(2):
            for h in range(A_HEADS):
                kv, g = divmod(h, A_GROUP)
                bias_scr[var, kv, g * BLOCK:(g + 1) * BLOCK, :] = _lookup_bias(
                    table_ref, bucket_ref[var], h)

    for sb in range(nsub):
        rows = slice(sb * BLOCK, (sb + 1) * BLOCK)
        prev_rows = slice((sb - 1) * BLOCK, sb * BLOCK)
        b_ref = bias_scr.at[jnp.minimum(i, 1) if sb == 0 else 1]
        outs = []
        for kv in range(A_KV_HEADS):
            cols = slice(kv * HEAD_DIM, (kv + 1) * HEAD_DIM)
            if sb == 0:
                k_prev, v_prev = kap_ref[:, cols], vap_ref[:, cols]
            else:
                k_prev, v_prev = ka_ref[prev_rows, cols], va_ref[prev_rows, cols]
            kk = jnp.concatenate([k_prev, ka_ref[rows, cols]], axis=0)
            vv = jnp.concatenate([v_prev, va_ref[rows, cols]], axis=0)
            heads = [kv * A_GROUP + g for g in range(A_GROUP)]
            q3 = jnp.concatenate(
                [qa_ref[rows, h * HEAD_DIM:(h + 1) * HEAD_DIM] for h in heads], axis=0)
            s = _dot_nt(q3, kk) + b_ref[kv]
            sink = jnp.concatenate(
                [jnp.full((BLOCK, 1), sink_ref[h] * LOG2E, F32) for h in heads], axis=0)
            p, _, den = _softmax_parts(s, sink)
            o = jnp.dot(p.astype(BF16), vv, preferred_element_type=F32) / den
            outs += [o[g * BLOCK:(g + 1) * BLOCK] for g in range(A_GROUP)]
        oa = jnp.concatenate(outs, axis=1)
        ya_ref[rows, :] = _rms(oa, ga_ref[...]).astype(BF16)

    outs = []
    for h in range(M_HEADS):
        cols = slice(h * HEAD_DIM, (h + 1) * HEAD_DIM)
        s = _dot_nt(qm_ref[:, cols], mk_ref[:, cols])
        p, _, den = _softmax_parts(s)
        outs.append(jnp.dot(p.astype(BF16), mv_ref[:, cols], preferred_element_type=F32) / den)
    ym_ref[...] = _rms(jnp.concatenate(outs, axis=1), gm_ref[...]).astype(BF16)


def _attn_am(qa, ka, va, qm, mk, mv, sinks, table, buckets, g_a, g_m, tq):
    s_len = qa.shape[0]
    nsub = tq // BLOCK
    mem_len = mk.shape[0]
    prev = lambda i: (jnp.maximum(i * nsub - 1, 0), 0)
    return pl.pallas_call(
        functools.partial(_attn_am_kernel, nsub=nsub),
        grid=(s_len // tq,),
        in_specs=[
            pl.BlockSpec(memory_space=pltpu.SMEM),
            pl.BlockSpec(memory_space=pltpu.SMEM),
            pl.BlockSpec(buckets.shape, lambda i: (0, 0, 0)),
            pl.BlockSpec((tq, A_Q), lambda i: (i, 0)),
            pl.BlockSpec((tq, A_KV), lambda i: (i, 0)),
            pl.BlockSpec((BLOCK, A_KV), prev),
            pl.BlockSpec((tq, A_KV), lambda i: (i, 0)),
            pl.BlockSpec((BLOCK, A_KV), prev),
            pl.BlockSpec((tq, M_W), lambda i: (i, 0)),
            pl.BlockSpec((mem_len, M_W), lambda i: (0, 0)),
            pl.BlockSpec((mem_len, M_W), lambda i: (0, 0)),
            pl.BlockSpec((1, A_Q), lambda i: (0, 0)),
            pl.BlockSpec((1, M_W), lambda i: (0, 0)),
        ],
        out_specs=[pl.BlockSpec((tq, A_Q), lambda i: (i, 0)),
                   pl.BlockSpec((tq, M_W), lambda i: (i, 0))],
        out_shape=[jax.ShapeDtypeStruct((s_len, A_Q), BF16),
                   jax.ShapeDtypeStruct((s_len, M_W), BF16)],
        scratch_shapes=[pltpu.VMEM((2, A_KV_HEADS, A_GROUP * BLOCK, 2 * BLOCK), F32)],
        compiler_params=_params(1),
        name="attn_am",
    )(sinks, table, buckets, qa, ka, ka, va, va, qm, mk, mv,
      g_a.reshape(1, A_Q), g_m.reshape(1, M_W))


def _build_bias_b(table_ref, bucket_ref, bias_scr):
    for var in range(2):
        for h in range(B_HEADS):
            bias_scr[var, h] = _lookup_bias(table_ref, bucket_ref[var], A_HEADS + h)


def _attn_b_tiles(qw_ref, kvw_ref, kvw_prev_ref, bias_scr, first_var, o_ref, lse_ref, nsub):
    ns = qw_ref.shape[0]
    per = BLOCK // ns
    half = B_HEADS // 2
    lane = lax.broadcasted_iota(jnp.int32, (BLOCK, HEAD_DIM), 1)

    def block(ref, sb, cols):
        return jnp.concatenate([ref[st, sb * per:(sb + 1) * per, cols] for st in range(ns)], axis=0)

    unpacked = {}
    for sb in range(nsub):
        b_ref = bias_scr.at[first_var if sb == 0 else 1]
        lse_tile = jnp.zeros((BLOCK, HEAD_DIM), F32)
        for h in range(B_HEADS):
            cols = slice(h * HEAD_DIM, (h + 1) * HEAD_DIM)
            qw = block(qw_ref, sb, slice((h % half) * HEAD_DIM, (h % half + 1) * HEAD_DIM))
            q = _unpack_hi(qw) if h < half else _unpack_lo(qw)
            if sb == 0:
                w_prev = block(kvw_prev_ref, 0, cols)
                k_prev, v_prev = _unpack_hi(w_prev), _unpack_lo(w_prev)
            else:
                k_prev, v_prev = unpacked.pop((sb - 1, h))
            w_cur = block(kvw_ref, sb, cols)
            k_cur, v_cur = _unpack_hi(w_cur), _unpack_lo(w_cur)
            unpacked[sb, h] = (k_cur, v_cur)
            kk = jnp.concatenate([k_prev, k_cur], axis=0)
            vv = jnp.concatenate([v_prev, v_cur], axis=0)
            s = _dot_nt(q, kk) + b_ref[h]
            p, m, den = _softmax_parts(s)
            o = jnp.dot(p.astype(BF16), vv, preferred_element_type=F32) / den
            for st in range(ns):
                o_ref[st, sb * per:(sb + 1) * per, cols] = o[st * per:(st + 1) * per]
            lse_tile = jnp.where(lane == h, m + jnp.log2(den), lse_tile)
        for st in range(ns):
            lse_ref[st, sb * per:(sb + 1) * per, :] = lse_tile[st * per:(st + 1) * per]


def _attn_b_nat_kernel(table_ref, bucket_ref, qw_ref, kvw_ref, kvwp_ref, o_ref, lse_ref, bias_scr,
                       *, nsub):
    i = pl.program_id(0)

    @pl.when(i == 0)
    def _():
        _build_bias_b(table_ref, bucket_ref, bias_scr)

    _attn_b_tiles(qw_ref, kvw_ref, kvwp_ref, bias_scr, jnp.minimum(i, 1), o_ref, lse_ref, nsub)


def _attn_b_dil_kernel(table_ref, bucket_ref, qw_hbm, kvw_hbm, o_hbm, lse_hbm,
                       qbuf, kvbuf, obuf, lbuf, carry, bias_scr, in_sems, out_sems,
                       *, nsub, steps_per_class, dil):
    s, n_steps = pl.program_id(0), pl.num_programs(0)
    ns = qbuf.shape[1]
    rows_per = qbuf.shape[2]
    slot = lax.rem(s, 2)

    def class_rows(step):
        c, i = lax.div(step, steps_per_class), lax.rem(step, steps_per_class)
        return [(pl.ds(i * rows_per, rows_per), c + st * dil) for st in range(ns)]

    def in_copies(step, sl):
        return [pltpu.make_async_copy(hbm.at[rows, col, :], buf.at[sl, st], in_sems.at[a, st, sl])
                for a, (hbm, buf) in enumerate(((qw_hbm, qbuf), (kvw_hbm, kvbuf)))
                for st, (rows, col) in enumerate(class_rows(step))]

    def out_copies(step, sl):
        return [pltpu.make_async_copy(buf.at[sl, st], hbm.at[rows, col, :], out_sems.at[a, st, sl])
                for a, (hbm, buf) in enumerate(((o_hbm, obuf), (lse_hbm, lbuf)))
                for st, (rows, col) in enumerate(class_rows(step))]

    @pl.when(s == 0)
    def _():
        for cp in in_copies(0, 0):
            cp.start()
        _build_bias_b(table_ref, bucket_ref, bias_scr)
        carry[...] = jnp.zeros(carry.shape, U32)

    @pl.when(s + 1 < n_steps)
    def _():
        for cp in in_copies(s + 1, 1 - slot):
            cp.start()

    for cp in in_copies(s, slot):
        cp.wait()

    @pl.when(s >= 2)
    def _():
        for cp in out_copies(s - 2, slot):
            cp.wait()

    first_var = jnp.minimum(lax.rem(s, steps_per_class), 1)
    _attn_b_tiles(qbuf.at[slot], kvbuf.at[slot], carry, bias_scr, first_var,
                  obuf.at[slot], lbuf.at[slot], nsub)
    carry[...] = kvbuf[slot, :, rows_per - carry.shape[1]:, :]

    for cp in out_copies(s, slot):
        cp.start()

    @pl.when(s == n_steps - 1)
    def _():
        for cp in out_copies(s - 1, 1 - slot) + out_copies(s, slot):
            cp.wait()


def _attn_b_nat(qw, kvw, table, buckets, tq):
    s_len = qw.shape[0]
    nsub = tq // BLOCK
    cur = lambda i: (0, i, 0)
    prev = lambda i: (0, jnp.maximum(i * nsub - 1, 0), 0)
    o, lse = pl.pallas_call(
        functools.partial(_attn_b_nat_kernel, nsub=nsub),
        grid=(s_len // tq,),
        in_specs=[
            pl.BlockSpec(memory_space=pltpu.SMEM),
            pl.BlockSpec(buckets.shape, lambda i: (0, 0, 0)),
            pl.BlockSpec((1, tq, B_QW), cur),
            pl.BlockSpec((1, tq, B_W), cur),
            pl.BlockSpec((1, BLOCK, B_W), prev),
        ],
        out_specs=[pl.BlockSpec((1, tq, B_W), cur), pl.BlockSpec((1, tq, HEAD_DIM), cur)],
        out_shape=[jax.ShapeDtypeStruct((1, s_len, B_W), F32),
                   jax.ShapeDtypeStruct((1, s_len, HEAD_DIM), F32)],
        scratch_shapes=[pltpu.VMEM((2, B_HEADS, BLOCK, 2 * BLOCK), F32)],
        compiler_params=_params(1),
        name="attn_b_d1",
    )(table, buckets, qw[None], kvw[None], kvw[None])
    return o[0], lse[0]


def _streams(dil):
    return 1 if dil == 1 else max(1, SUBLANES // dil)


def _attn_b_dil(qw, kvw, table, buckets, dil, tq):
    s_len = qw.shape[0]
    ns = _streams(dil)
    n = dil * ns
    rows_per = tq // ns
    steps_per_class = s_len // dil // tq
    view = lambda t: t.reshape(s_len // n, n, t.shape[-1])
    any_spec = pl.BlockSpec(memory_space=pl.ANY)
    o, lse = pl.pallas_call(
        functools.partial(_attn_b_dil_kernel, nsub=tq // BLOCK, steps_per_class=steps_per_class,
                          dil=dil),
        grid=(dil * steps_per_class,),
        in_specs=[pl.BlockSpec(memory_space=pltpu.SMEM),
                  pl.BlockSpec(buckets.shape, lambda s: (0, 0, 0)), any_spec, any_spec],
        out_specs=[any_spec, any_spec],
        out_shape=[jax.ShapeDtypeStruct((s_len // n, n, B_W), F32),
                   jax.ShapeDtypeStruct((s_len // n, n, HEAD_DIM), F32)],
        scratch_shapes=[
            pltpu.VMEM((2, ns, rows_per, B_QW), U32),
            pltpu.VMEM((2, ns, rows_per, B_W), U32),
            pltpu.VMEM((2, ns, rows_per, B_W), F32),
            pltpu.VMEM((2, ns, rows_per, HEAD_DIM), F32),
            pltpu.VMEM((ns, BLOCK // ns, B_W), U32),
            pltpu.VMEM((2, B_HEADS, BLOCK, 2 * BLOCK), F32),
            pltpu.SemaphoreType.DMA((2, ns, 2)),
            pltpu.SemaphoreType.DMA((2, ns, 2)),
        ],
        compiler_params=_params(1),
        name=f"attn_b_d{dil}",
    )(table, buckets, view(qw), view(kvw))
    return o.reshape(s_len, B_W), lse.reshape(s_len, HEAD_DIM)


def _out_kernel(x_ref, ya_ref, ym_ref, o1_ref, o2_ref, o3_ref, l1_ref, l2_ref, l3_ref,
                gb_ref, w_ref, gpost_ref, gpre_ref, x1_ref, h2_ref, *, chunk):
    o_refs = (o1_ref, o2_ref, o3_ref)
    for c in range(x_ref.shape[0] // chunk):
        rows = slice(c * chunk, (c + 1) * chunk)
        obs = []
        for h in range(B_HEADS):
            cols = slice(h * HEAD_DIM, (h + 1) * HEAD_DIM)
            ls = [l_ref[rows, h:h + 1] for l_ref in (l1_ref, l2_ref, l3_ref)]
            mx = jnp.maximum(jnp.maximum(ls[0], ls[1]), ls[2])
            es = [jnp.exp2(l - mx) for l in ls]
            inv = 1.0 / (es[0] + es[1] + es[2])
            obs.append(sum((e * inv) * o_ref[rows, cols] for e, o_ref in zip(es, o_refs)))
        yb = _rms(jnp.concatenate(obs, axis=1), gb_ref[...]).astype(BF16)
        y_in = jnp.concatenate([ya_ref[rows, :], yb, ym_ref[rows, :]], axis=1)
        y = jnp.dot(y_in, w_ref[...], preferred_element_type=F32)
        x1 = x_ref[rows, :] + _rms(y, gpost_ref[...])
        x1_ref[rows, :] = x1
        h2_ref[rows, :] = _rms(x1, gpre_ref[...]).astype(BF16)


def _out_proj(x, ya, ym, obs, lses, g_b, w_out, g_post, g_pre, tm, chunk):
    s_len, d = x.shape
    row = lambda width: pl.BlockSpec((tm, width), lambda i: (i, 0))
    vec = lambda width: pl.BlockSpec((1, width), lambda i: (0, 0))
    return pl.pallas_call(
        functools.partial(_out_kernel, chunk=chunk),
        grid=(s_len // tm,),
        in_specs=[row(d), row(A_Q), row(M_W), row(B_W), row(B_W), row(B_W),
                  row(HEAD_DIM), row(HEAD_DIM), row(HEAD_DIM), vec(B_W),
                  pl.BlockSpec(w_out.shape, lambda i: (0, 0), pipeline_mode=pl.Buffered(1)),
                  vec(d), vec(d)],
        out_specs=[row(d), row(d)],
        out_shape=[jax.ShapeDtypeStruct((s_len, d), F32), jax.ShapeDtypeStruct((s_len, d), BF16)],
        compiler_params=_params(1),
        name="out_proj",
    )(x, ya, ym, *obs, *lses, g_b.reshape(1, B_W), w_out, g_post.reshape(1, d),
      g_pre.reshape(1, d))


def _shift_rows(u, carry, shift):
    rows, n = u.shape
    above = jnp.concatenate([carry, u[:-SUBLANES]], axis=0).reshape(rows // SUBLANES, SUBLANES, n)
    from_above = lax.broadcasted_iota(jnp.int32, (1, SUBLANES, n), 1) >= SUBLANES - shift
    mixed = jnp.where(from_above, above, u.reshape(rows // SUBLANES, SUBLANES, n))
    return pltpu.roll(mixed, shift, axis=1).reshape(rows, n)


def _causal_conv(u, carry, cw, cb):
    return (cw[0:1] * _shift_rows(u, carry, 2) + cw[1:2] * _shift_rows(u, carry, 1)
            + cw[2:3] * u + cb)


def _ffn_kernel(h_ref, wg_ref, wv_ref, cwg_ref, cwv_ref, cbg_ref, cbv_ref, wd_ref, x_hbm, g_ref,
                o_ref, carry_g, carry_v, x_buf, x_sem, *, chunks):
    i, j = pl.program_id(0), pl.program_id(1)
    tm = x_buf.shape[0]
    x_copy = pltpu.make_async_copy(x_hbm.at[pl.ds(i * tm, tm), :], x_buf, x_sem)

    @pl.when(i == 0)
    def _():
        carry_g[j] = jnp.zeros(carry_g.shape[1:], F32)
        carry_v[j] = jnp.zeros(carry_v.shape[1:], F32)

    def d_ff_tile(first):
        prev_g, prev_v = carry_g[j], carry_v[j]
        starts = [sum(chunks[:c]) for c in range(len(chunks))]
        for start, size in zip(starts, chunks):
            rows = slice(start, start + size)
            h = h_ref[rows, :]
            ug = jnp.dot(h, wg_ref[...], preferred_element_type=F32)
            uv = jnp.dot(h, wv_ref[...], preferred_element_type=F32)
            cg = _causal_conv(ug, prev_g, cwg_ref[...], cbg_ref[...])
            cv = _causal_conv(uv, prev_v, cwv_ref[...], cbv_ref[...])
            prev_g, prev_v = ug[-SUBLANES:], uv[-SUBLANES:]
            act = (jax.nn.gelu(cg, approximate=True) * cv).astype(BF16)
            part = jnp.dot(act, wd_ref[...], preferred_element_type=F32)
            if first:
                o_ref[rows, :] = part
            else:
                o_ref[rows, :] += part
        carry_g[j] = prev_g
        carry_v[j] = prev_v

    @pl.when(j == 0)
    def _():
        x_copy.start()
        d_ff_tile(first=True)

    @pl.when(j > 0)
    def _():
        d_ff_tile(first=False)

    @pl.when(j == pl.num_programs(1) - 1)
    def _():
        x_copy.wait()
        o_ref[...] = x_buf[...] + _rms(o_ref[...], g_ref[...])


def _ffn(h2, x1, w_up, conv_w, conv_b, w_down, g_post, tf, chunks):
    s_len, d = x1.shape
    d_ff = w_down.shape[0]
    nf = d_ff // tf
    tm = sum(chunks)
    assert d_ff % tf == 0 and s_len % tm == 0
    return pl.pallas_call(
        functools.partial(_ffn_kernel, chunks=chunks),
        grid=(s_len // tm, nf),
        in_specs=[
            pl.BlockSpec((tm, d), lambda i, j: (i, 0)),
            pl.BlockSpec((d, tf), lambda i, j: (0, j)),
            pl.BlockSpec((d, tf), lambda i, j: (0, nf + j)),
            pl.BlockSpec((CONV_WIDTH, tf), lambda i, j: (0, j)),
            pl.BlockSpec((CONV_WIDTH, tf), lambda i, j: (0, nf + j)),
            pl.BlockSpec((1, tf), lambda i, j: (0, j)),
            pl.BlockSpec((1, tf), lambda i, j: (0, nf + j)),
            pl.BlockSpec((tf, d), lambda i, j: (j, 0)),
            pl.BlockSpec(memory_space=pl.ANY),
            pl.BlockSpec((1, d), lambda i, j: (0, 0)),
        ],
        out_specs=pl.BlockSpec((tm, d), lambda i, j: (i, 0)),
        out_shape=jax.ShapeDtypeStruct((s_len, d), F32),
        scratch_shapes=[pltpu.VMEM((nf, SUBLANES, tf), F32), pltpu.VMEM((nf, SUBLANES, tf), F32),
                        pltpu.VMEM((tm, d), F32), pltpu.SemaphoreType.DMA(())],
        compiler_params=_params(2),
        name="conv_glu_ffn",
    )(h2, w_up, w_up, conv_w, conv_w, conv_b.reshape(1, -1), conv_b.reshape(1, -1), w_down,
      x1, g_post.reshape(1, d))


def _rel_bucket(dist):
    n = jnp.maximum(dist, 0)
    nf = jnp.maximum(n, 1).astype(F32)
    large = REL_MAX_EXACT + (jnp.log(nf / REL_MAX_EXACT)
                             / math.log(REL_MAX_DISTANCE / REL_MAX_EXACT)
                             * (REL_BUCKETS - REL_MAX_EXACT)).astype(jnp.int32)
    large = jnp.minimum(large, REL_BUCKETS - 1)
    return jnp.where(n < REL_MAX_EXACT, n, large)


def _bucket_blocks(dilation, max_dist, streams=1):
    per = BLOCK // streams
    idx = jnp.arange(BLOCK)
    member = streams * (idx % per) + idx // per
    key = jnp.concatenate([member, BLOCK + member])
    rel = member[:, None] + BLOCK - key[None, :]
    valid = (rel >= 0) & (rel <= max_dist)
    normal = jnp.where(valid, _rel_bucket(rel * dilation), -1)
    first = jnp.where((jnp.arange(2 * BLOCK) >= BLOCK)[None, :], normal, -1)
    return jnp.stack([first, normal]).astype(jnp.int32)


def kernel(x, mem, rel_bias, g_mix_pre, w_in, a_sinks, g_mem, w_mem_kv, g_out, w_out, g_mix_post,
           g_ffn_pre, w_up, conv_w, conv_b, w_down, g_ffn_post):
    bn, s_len, d = x.shape
    assert bn == 1
    depth = w_in.shape[0]
    xs = x.reshape(s_len, d)
    mems = mem.reshape(mem.shape[1], d)

    buckets_a = _bucket_blocks(1, A_WINDOW - 1)
    buckets_b = [_bucket_blocks(dil, window // dil, _streams(dil)) for window, dil in B_PATTERNS]

    off_ka, off_va, off_qb = A_Q, A_Q + A_KV, A_Q + 2 * A_KV
    off_kb, off_vb, off_qm = off_qb + B_W, off_qb + 2 * B_W, off_qb + 3 * B_W
    proj_outs = (((0,), A_Q, Q_SCALE), ((off_ka,), A_KV, 1.0), ((off_va,), A_KV, 1.0),
                 ((off_qb, off_qb + B_QW), B_QW, Q_SCALE), ((off_kb, off_vb), B_W, 1.0),
                 ((off_qm,), M_W, Q_SCALE))
    mem_outs = (((0,), M_W, 1.0), ((M_W,), M_W, 1.0))

    w_in_l = w_in[0].astype(BF16)
    tq = 2048

    for l in range(depth):
        cast = [(w_up, l), (w_down, l), (w_out, l)] + ([(w_in, l + 1)] if l + 1 < depth else [])
        qa, ka, va, qbw, kvbw, qm, w_up_l, w_down_l, w_out_l, *w_in_next = _norm_proj(
            xs, g_mix_pre[l], w_in_l, proj_outs, tm=512, cast=cast)
        w_in_l = w_in_next[0] if w_in_next else None
        mk, mv = _norm_proj(mems, g_mem[l], w_mem_kv[l].astype(BF16), mem_outs, tm=mems.shape[0])
        ya, ym = _attn_am(qa, ka, va, qm, mk, mv, a_sinks[l], rel_bias, buckets_a,
                          g_out[l, :A_Q], g_out[l, A_Q + B_W:], tq=tq)
        obs, lses = [], []
        for (window, dil), buckets in zip(B_PATTERNS, buckets_b):
            if dil == 1:
                o, lse = _attn_b_nat(qbw, kvbw, rel_bias, buckets, tq=tq)
            else:
                o, lse = _attn_b_dil(qbw, kvbw, rel_bias, buckets, dil, tq=min(tq, s_len // dil))
            obs.append(o)
            lses.append(lse)
        x1, h2 = _out_proj(xs, ya, ym, obs, lses, g_out[l, A_Q:A_Q + B_W], w_out_l,
                           g_mix_post[l], g_ffn_pre[l], tm=512, chunk=128)
        xs = _ffn(h2, x1, w_up_l, conv_w[l], conv_b[l], w_down_l, g_ffn_post[l],
                  tf=512, chunks=(640, 384))
    return xs.reshape(bn, s_len, d)
```

```python
import functools
import math

import jax
import jax.numpy as jnp
from jax import lax
from jax.experimental import pallas as pl
from jax.experimental.pallas import tpu as pltpu

F32 = jnp.float32
BF16 = jnp.bfloat16
U32 = jnp.uint32

HEAD_DIM = 128
BLOCK = 128
A_HEADS, A_KV_HEADS, A_GROUP = 6, 2, 3
A_WINDOW = 128
B_HEADS = 6
B_PATTERNS = ((128, 1), (512, 4), (2048, 16))
M_HEADS = 4
REL_BUCKETS, REL_MAX_EXACT, REL_MAX_DISTANCE = 32, 16, 2048
CONV_WIDTH = 3
EPS = 1e-6
NEG = -1e30

A_Q = A_HEADS * HEAD_DIM
A_KV = A_KV_HEADS * HEAD_DIM
B_W = B_HEADS * HEAD_DIM
M_W = M_HEADS * HEAD_DIM
B_QW = B_W // 2
LOG2E = math.log2(math.e)
Q_SCALE = HEAD_DIM ** -0.5 * LOG2E

VMEM_LIMIT_BYTES = 56 * 1024 * 1024
SUBLANES = 8
BF16_ROWS = 16


def _params(n_axes):
    return pltpu.CompilerParams(dimension_semantics=("arbitrary",) * n_axes,
                                vmem_limit_bytes=VMEM_LIMIT_BYTES)


def _rms(x, g):
    return x * lax.rsqrt(jnp.mean(x * x, axis=-1, keepdims=True) + EPS) * g


def _dot_nt(a, b):
    return lax.dot_general(a, b, (((1,), (1,)), ((), ())), preferred_element_type=F32)


def _softmax_parts(s, sink=None):
    m = jnp.max(s, axis=-1, keepdims=True)
    if sink is not None:
        m = jnp.maximum(m, sink)
    p = jnp.exp2(s - m)
    den = jnp.sum(p, axis=-1, keepdims=True)
    if sink is not None:
        den = den + jnp.exp2(sink - m)
    return p, m, den


def _pack_pair(hi, lo):
    hi_bits = lax.bitcast_convert_type(hi.astype(BF16).astype(F32), U32)
    lo_bits = lax.bitcast_convert_type(lo.astype(BF16).astype(F32), U32)
    return hi_bits | (lo_bits >> 16)


def _unpack_hi(words):
    return lax.bitcast_convert_type(words & jnp.uint32(0xFFFF0000), F32).astype(BF16)


def _unpack_lo(words):
    return lax.bitcast_convert_type(words << 16, F32).astype(BF16)


def _proj_kernel(x_ref, g_ref, w_ref, *refs, outs, n_cast):
    cast_in, o_refs = refs[:n_cast], refs[n_cast:n_cast + len(outs)]
    cast_out = refs[n_cast + len(outs):]
    h = _rms(x_ref[...], g_ref[...]).astype(BF16)

    def cols(off, width, scale):
        acc = jnp.dot(h, w_ref[:, off:off + width], preferred_element_type=F32)
        return acc if scale == 1.0 else acc * scale

    for o_ref, (offs, width, scale) in zip(o_refs, outs):
        if len(offs) == 1:
            o_ref[...] = cols(offs[0], width, scale).astype(BF16)
        else:
            o_ref[...] = _pack_pair(cols(offs[0], width, scale), cols(offs[1], width, scale))

    for src, dst in zip(cast_in, cast_out):
        dst[...] = src[...].astype(BF16)


def _norm_proj(x, g, w, outs, tm, cast=()):
    rows, d = x.shape
    n = w.shape[1]
    steps = rows // tm
    assert rows % tm == 0 and all(a.shape[1] % (BF16_ROWS * steps) == 0 for a, _ in cast)
    return pl.pallas_call(
        functools.partial(_proj_kernel, outs=outs, n_cast=len(cast)),
        grid=(steps,),
        in_specs=[
            pl.BlockSpec((tm, d), lambda i: (i, 0)),
            pl.BlockSpec((1, d), lambda i: (0, 0)),
            pl.BlockSpec((d, n), lambda i: (0, 0), pipeline_mode=pl.Buffered(1)),
        ] + [pl.BlockSpec((None, a.shape[1] // steps, a.shape[2]),
                          functools.partial(lambda layer, i: (layer, i, 0), layer))
             for a, layer in cast],
        out_specs=[pl.BlockSpec((tm, width), lambda i: (i, 0)) for _, width, _ in outs]
        + [pl.BlockSpec((a.shape[1] // steps, a.shape[2]), lambda i: (i, 0)) for a, _ in cast],
        out_shape=[jax.ShapeDtypeStruct((rows, width), BF16 if len(offs) == 1 else U32)
                   for offs, width, _ in outs]
        + [jax.ShapeDtypeStruct(a.shape[1:], BF16) for a, _ in cast],
        compiler_params=_params(1),
        name="norm_proj",
    )(x, g.reshape(1, d), w, *(a for a, _ in cast))


def _lookup_bias(table_ref, bucket, head):
    def body(b, acc):
        return jnp.where(bucket == b, table_ref[b, head] * LOG2E, acc)
    return lax.fori_loop(0, REL_BUCKETS, body, jnp.full(bucket.shape, NEG, F32))


def _attn_am_kernel(sink_ref, table_ref, bucket_ref, qa_ref, ka_ref, kap_ref, va_ref, vap_ref,
                    qm_ref, mk_ref, mv_ref, ga_ref, gm_ref, ya_ref, ym_ref, bias_scr, *, nsub):
    i = pl.program_id(0)

    @pl.when(i == 0)
    def _():
        for var in range(2):
            for h in range(A_HEADS):
                kv, g = divmod(h, A_GROUP)
                bias_scr[var, kv, g * BLOCK:(g + 1) * BLOCK, :] = _lookup_bias(
                    table_ref, bucket_ref[var], h)

    for sb in range(nsub):
        rows = slice(sb * BLOCK, (sb + 1) * BLOCK)
        prev_rows = slice((sb - 1) * BLOCK, sb * BLOCK)
        b_ref = bias_scr.at[jnp.minimum(i, 1) if sb == 0 else 1]
        outs = []
        for kv in range(A_KV_HEADS):
            cols = slice(kv * HEAD_DIM, (kv + 1) * HEAD_DIM)
            if sb == 0:
                k_prev, v_prev = kap_ref[:, cols], vap_ref[:, cols]
            else:
                k_prev, v_prev = ka_ref[prev_rows, cols], va_ref[prev_rows, cols]
            kk = jnp.concatenate([k_prev, ka_ref[rows, cols]], axis=0)
            vv = jnp.concatenate([v_prev, va_ref[rows, cols]], axis=0)
            heads = [kv * A_GROUP + g for g in range(A_GROUP)]
            q3 = jnp.concatenate(
                [qa_ref[rows, h * HEAD_DIM:(h + 1) * HEAD_DIM] for h in heads], axis=0)
            s = _dot_nt(q3, kk) + b_ref[kv]
            sink = jnp.concatenate(
                [jnp.full((BLOCK, 1), sink_ref[h] * LOG2E, F32) for h in heads], axis=0)
            p, _, den = _softmax_parts(s, sink)
            o = jnp.dot(p.astype(BF16), vv, preferred_element_type=F32) / den
            outs += [o[g * BLOCK:(g + 1) * BLOCK] for g in range(A_GROUP)]
        oa = jnp.concatenate(outs, axis=1)
        ya_ref[rows, :] = _rms(oa, ga_ref[...]).astype(BF16)

    outs = []
    for h in range(M_HEADS):
        cols = slice(h * HEAD_DIM, (h + 1) * HEAD_DIM)
        s = _dot_nt(qm_ref[:, cols], mk_ref[:, cols])
        p, _, den = _softmax_parts(s)
        outs.append(jnp.dot(p.astype(BF16), mv_ref[:, cols], preferred_element_type=F32) / den)
    ym_ref[...] = _rms(jnp.concatenate(outs, axis=1), gm_ref[...]).astype(BF16)


def _attn_am(qa, ka, va, qm, mk, mv, sinks, table, buckets, g_a, g_m, tq):
    s_len = qa.shape[0]
    nsub = tq // BLOCK
    mem_len = mk.shape[0]
    prev = lambda i: (jnp.maximum(i * nsub - 1, 0), 0)
    return pl.pallas_call(
        functools.partial(_attn_am_kernel, nsub=nsub),
        grid=(s_len // tq,),
        in_specs=[
            pl.BlockSpec(memory_space=pltpu.SMEM),
            pl.BlockSpec(memory_space=pltpu.SMEM),
            pl.BlockSpec(buckets.shape, lambda i: (0, 0, 0)),
            pl.BlockSpec((tq, A_Q), lambda i: (i, 0)),
            pl.BlockSpec((tq, A_KV), lambda i: (i, 0)),
            pl.BlockSpec((BLOCK, A_KV), prev),
            pl.BlockSpec((tq, A_KV), lambda i: (i, 0)),
            pl.BlockSpec((BLOCK, A_KV), prev),
            pl.BlockSpec((tq, M_W), lambda i: (i, 0)),
            pl.BlockSpec((mem_len, M_W), lambda i: (0, 0)),
            pl.BlockSpec((mem_len, M_W), lambda i: (0, 0)),
            pl.BlockSpec((1, A_Q), lambda i: (0, 0)),
            pl.BlockSpec((1, M_W), lambda i: (0, 0)),
        ],
        out_specs=[pl.BlockSpec((tq, A_Q), lambda i: (i, 0)),
                   pl.BlockSpec((tq, M_W), lambda i: (i, 0))],
        out_shape=[jax.ShapeDtypeStruct((s_len, A_Q), BF16),
                   jax.ShapeDtypeStruct((s_len, M_W), BF16)],
        scratch_shapes=[pltpu.VMEM((2, A_KV_HEADS, A_GROUP * BLOCK, 2 * BLOCK), F32)],
        compiler_params=_params(1),
        name="attn_am",
    )(sinks, table, buckets, qa, ka, ka, va, va, qm, mk, mv,
      g_a.reshape(1, A_Q), g_m.reshape(1, M_W))


def _build_bias_b(table_ref, bucket_ref, bias_scr):
    for var in range(2):
        for h in range(B_HEADS):
            bias_scr[var, h] = _lookup_bias(table_ref, bucket_ref[var], A_HEADS + h)


def _attn_b_tiles(qw_ref, kvw_ref, kvw_prev_ref, bias_scr, first_var, o_ref, lse_ref, nsub):
    ns = qw_ref.shape[0]
    per = BLOCK // ns
    half = B_HEADS // 2
    lane = lax.broadcasted_iota(jnp.int32, (BLOCK, HEAD_DIM), 1)

    def block(ref, sb, cols):
        return jnp.concatenate([ref[st, sb * per:(sb + 1) * per, cols] for st in range(ns)], axis=0)

    unpacked = {}
    for sb in range(nsub):
        b_ref = bias_scr.at[first_var if sb == 0 else 1]
        lse_tile = jnp.zeros((BLOCK, HEAD_DIM), F32)
        for h in range(B_HEADS):
            cols = slice(h * HEAD_DIM, (h + 1) * HEAD_DIM)
            qw = block(qw_ref, sb, slice((h % half) * HEAD_DIM, (h % half + 1) * HEAD_DIM))
            q = _unpack_hi(qw) if h < half else _unpack_lo(qw)
            if sb == 0:
                w_prev = block(kvw_prev_ref, 0, cols)
                k_prev, v_prev = _unpack_hi(w_prev), _unpack_lo(w_prev)
            else:
                k_prev, v_prev = unpacked.pop((sb - 1, h))
            w_cur = block(kvw_ref, sb, cols)
            k_cur, v_cur = _unpack_hi(w_cur), _unpack_lo(w_cur)
            unpacked[sb, h] = (k_cur, v_cur)
            kk = jnp.concatenate([k_prev, k_cur], axis=0)
            vv = jnp.concatenate([v_prev, v_cur], axis=0)
            s = _dot_nt(q, kk) + b_ref[h]
            p, m, den = _softmax_parts(s)
            o = jnp.dot(p.astype(BF16), vv, preferred_element_type=F32) / den
            for st in range(ns):
                o_ref[st, sb * per:(sb + 1) * per, cols] = o[st * per:(st + 1) * per]
            lse_tile = jnp.where(lane == h, m + jnp.log2(den), lse_tile)
        for st in range(ns):
            lse_ref[st, sb * per:(sb + 1) * per, :] = lse_tile[st * per:(st + 1) * per]


def _attn_b_nat_kernel(table_ref, bucket_ref, qw_ref, kvw_ref, kvwp_ref, o_ref, lse_ref, bias_scr,
                       *, nsub):
    i = pl.program_id(0)

    @pl.when(i == 0)
    def _():
        _build_bias_b(table_ref, bucket_ref, bias_scr)

    _attn_b_tiles(qw_ref, kvw_ref, kvwp_ref, bias_scr, jnp.minimum(i, 1), o_ref, lse_ref, nsub)


def _attn_b_dil_kernel(table_ref, bucket_ref, qw_hbm, kvw_hbm, o_hbm, lse_hbm,
                       qbuf, kvbuf, obuf, lbuf, carry, bias_scr, in_sems, out_sems,
                       *, nsub, steps_per_class, dil):
    s, n_steps = pl.program_id(0), pl.num_programs(0)
    ns = qbuf.shape[1]
    rows_per = qbuf.shape[2]
    slot = lax.rem(s, 2)

    def class_rows(step):
        c, i = lax.div(step, steps_per_class), lax.rem(step, steps_per_class)
        return [(pl.ds(i * rows_per, rows_per), c + st * dil) for st in range(ns)]

    def in_copies(step, sl):
        return [pltpu.make_async_copy(hbm.at[rows, col, :], buf.at[sl, st], in_sems.at[a, st, sl])
                for a, (hbm, buf) in enumerate(((qw_hbm, qbuf), (kvw_hbm, kvbuf)))
                for st, (rows, col) in enumerate(class_rows(step))]

    def out_copies(step, sl):
        return [pltpu.make_async_copy(buf.at[sl, st], hbm.at[rows, col, :], out_sems.at[a, st, sl])
                for a, (hbm, buf) in enumerate(((o_hbm, obuf), (lse_hbm, lbuf)))
                for st, (rows, col) in enumerate(class_rows(step))]

    @pl.when(s == 0)
    def _():
        for cp in in_copies(0, 0):
            cp.start()
        _build_bias_b(table_ref, bucket_ref, bias_scr)
        carry[...] = jnp.zeros(carry.shape, U32)

    @pl.when(s + 1 < n_steps)
    def _():
        for cp in in_copies(s + 1, 1 - slot):
            cp.start()

    for cp in in_copies(s, slot):
        cp.wait()

    @pl.when(s >= 2)
    def _():
        for cp in out_copies(s - 2, slot):
            cp.wait()

    first_var = jnp.minimum(lax.rem(s, steps_per_class), 1)
    _attn_b_tiles(qbuf.at[slot], kvbuf.at[slot], carry, bias_scr, first_var,
                  obuf.at[slot], lbuf.at[slot], nsub)
    carry[...] = kvbuf[slot, :, rows_per - carry.shape[1]:, :]

    for cp in out_copies(s, slot):
        cp.start()

    @pl.when(s == n_steps - 1)
    def _():
        for cp in out_copies(s - 1, 1 - slot) + out_copies(s, slot):
            cp.wait()


def _attn_b_nat(qw, kvw, table, buckets, tq):
    s_len = qw.shape[0]
    nsub = tq // BLOCK
    cur = lambda i: (0, i, 0)
    prev = lambda i: (0, jnp.maximum(i * nsub - 1, 0), 0)
    o, lse = pl.pallas_call(
        functools.partial(_attn_b_nat_kernel, nsub=nsub),
        grid=(s_len // tq,),
        in_specs=[
            pl.BlockSpec(memory_space=pltpu.SMEM),
            pl.BlockSpec(buckets.shape, lambda i: (0, 0, 0)),
            pl.BlockSpec((1, tq, B_QW), cur),
            pl.BlockSpec((1, tq, B_W), cur),
            pl.BlockSpec((1, BLOCK, B_W), prev),
        ],
        out_specs=[pl.BlockSpec((1, tq, B_W), cur), pl.BlockSpec((1, tq, HEAD_DIM), cur)],
        out_shape=[jax.ShapeDtypeStruct((1, s_len, B_W), F32),
                   jax.ShapeDtypeStruct((1, s_len, HEAD_DIM), F32)],
        scratch_shapes=[pltpu.VMEM((2, B_HEADS, BLOCK, 2 * BLOCK), F32)],
        compiler_params=_params(1),
        name="attn_b_d1",
    )(table, buckets, qw[None], kvw[None], kvw[None])
    return o[0], lse[0]


def _streams(dil):
    return 1 if dil == 1 else max(1, SUBLANES // dil)


def _attn_b_dil(qw, kvw, table, buckets, dil, tq):
    s_len = qw.shape[0]
    ns = _streams(dil)
    n = dil * ns
    rows_per = tq // ns
    steps_per_class = s_len // dil // tq
    view = lambda t: t.reshape(s_len // n, n, t.shape[-1])
    any_spec = pl.BlockSpec(memory_space=pl.ANY)
    o, lse = pl.pallas_call(
        functools.partial(_attn_b_dil_kernel, nsub=tq // BLOCK, steps_per_class=steps_per_class,
                          dil=dil),
        grid=(dil * steps_per_class,),
        in_specs=[pl.BlockSpec(memory_space=pltpu.SMEM),
                  pl.BlockSpec(buckets.shape, lambda s: (0, 0, 0)), any_spec, any_spec],
        out_specs=[any_spec, any_spec],
        out_shape=[jax.ShapeDtypeStruct((s_len // n, n, B_W), F32),
                   jax.ShapeDtypeStruct((s_len // n, n, HEAD_DIM), F32)],
        scratch_shapes=[
            pltpu.VMEM((2, ns, rows_per, B_QW), U32),
            pltpu.VMEM((2, ns, rows_per, B_W), U32),
            pltpu.VMEM((2, ns, rows_per, B_W), F32),
            pltpu.VMEM((2, ns, rows_per, HEAD_DIM), F32),
            pltpu.VMEM((ns, BLOCK // ns, B_W), U32),
            pltpu.VMEM((2, B_HEADS, BLOCK, 2 * BLOCK), F32),
            pltpu.SemaphoreType.DMA((2, ns, 2)),
            pltpu.SemaphoreType.DMA((2, ns, 2)),
        ],
        compiler_params=_params(1),
        name=f"attn_b_d{dil}",
    )(table, buckets, view(qw), view(kvw))
    return o.reshape(s_len, B_W), lse.reshape(s_len, HEAD_DIM)


def _out_kernel(x_ref, ya_ref, ym_ref, o1_ref, o2_ref, o3_ref, l1_ref, l2_ref, l3_ref,
                gb_ref, w_ref, gpost_ref, gpre_ref, x1_ref, h2_ref, *, chunk):
    o_refs = (o1_ref, o2_ref, o3_ref)
    for c in range(x_ref.shape[0] // chunk):
        rows = slice(c * chunk, (c + 1) * chunk)
        obs = []
        for h in range(B_HEADS):
            cols = slice(h * HEAD_DIM, (h + 1) * HEAD_DIM)
            ls = [l_ref[rows, h:h + 1] for l_ref in (l1_ref, l2_ref, l3_ref)]
            mx = jnp.maximum(jnp.maximum(ls[0], ls[1]), ls[2])
            es = [jnp.exp2(l - mx) for l in ls]
            inv = 1.0 / (es[0] + es[1] + es[2])
            obs.append(sum((e * inv) * o_ref[rows, cols] for e, o_ref in zip(es, o_refs)))
        yb = _rms(jnp.concatenate(obs, axis=1), gb_ref[...]).astype(BF16)
        y_in = jnp.concatenate([ya_ref[rows, :], yb, ym_ref[rows, :]], axis=1)
        y = jnp.dot(y_in, w_ref[...], preferred_element_type=F32)
        x1 = x_ref[rows, :] + _rms(y, gpost_ref[...])
        x1_ref[rows, :] = x1
        h2_ref[rows, :] = _rms(x1, gpre_ref[...]).astype(BF16)


def _out_proj(x, ya, ym, obs, lses, g_b, w_out, g_post, g_pre, tm, chunk):
    s_len, d = x.shape
    row = lambda width: pl.BlockSpec((tm, width), lambda i: (i, 0))
    vec = lambda width: pl.BlockSpec((1, width), lambda i: (0, 0))
    return pl.pallas_call(
        functools.partial(_out_kernel, chunk=chunk),
        grid=(s_len // tm,),
        in_specs=[row(d), row(A_Q), row(M_W), row(B_W), row(B_W), row(B_W),
                  row(HEAD_DIM), row(HEAD_DIM), row(HEAD_DIM), vec(B_W),
                  pl.BlockSpec(w_out.shape, lambda i: (0, 0), pipeline_mode=pl.Buffered(1)),
                  vec(d), vec(d)],
        out_specs=[row(d), row(d)],
        out_shape=[jax.ShapeDtypeStruct((s_len, d), F32), jax.ShapeDtypeStruct((s_len, d), BF16)],
        compiler_params=_params(1),
        name="out_proj",
    )(x, ya, ym, *obs, *lses, g_b.reshape(1, B_W), w_out, g_post.reshape(1, d),
      g_pre.reshape(1, d))


def _shift_rows(u, carry, shift):
    rows, n = u.shape
    above = jnp.concatenate([carry, u[:-SUBLANES]], axis=0).reshape(rows // SUBLANES, SUBLANES, n)
    from_above = lax.broadcasted_iota(jnp.int32, (1, SUBLANES, n), 1) >= SUBLANES - shift
    mixed = jnp.where(from_above, above, u.reshape(rows // SUBLANES, SUBLANES, n))
    return pltpu.roll(mixed, shift, axis=1).reshape(rows, n)


def _causal_conv(u, carry, cw, cb):
    return (cw[0:1] * _shift_rows(u, carry, 2) + cw[1:2] * _shift_rows(u, carry, 1)
            + cw[2:3] * u + cb)


def _ffn_kernel(h_ref, wg_ref, wv_ref, cwg_ref, cwv_ref, cbg_ref, cbv_ref, wd_ref, x_hbm, g_ref,
                o_ref, carry_g, carry_v, x_buf, x_sem, *, chunks):
    i, j = pl.program_id(0), pl.program_id(1)
    tm = x_buf.shape[0]
    x_copy = pltpu.make_async_copy(x_hbm.at[pl.ds(i * tm, tm), :], x_buf, x_sem)

    @pl.when(i == 0)
    def _():
        carry_g[j] = jnp.zeros(carry_g.shape[1:], F32)
        carry_v[j] = jnp.zeros(carry_v.shape[1:], F32)

    @pl.when(j == 0)
    def _():
        x_copy.start()
        o_ref[...] = jnp.zeros(o_ref.shape, F32)

    prev_g, prev_v = carry_g[j], carry_v[j]
    starts = [sum(chunks[:c]) for c in range(len(chunks))]
    for start, size in zip(starts, chunks):
        rows = slice(start, start + size)
        h = h_ref[rows, :]
        ug = jnp.dot(h, wg_ref[...], preferred_element_type=F32)
        uv = jnp.dot(h, wv_ref[...], preferred_element_type=F32)
        cg = _causal_conv(ug, prev_g, cwg_ref[...], cbg_ref[...])
        cv = _causal_conv(uv, prev_v, cwv_ref[...], cbv_ref[...])
        prev_g, prev_v = ug[-SUBLANES:], uv[-SUBLANES:]
        act = (jax.nn.gelu(cg, approximate=True) * cv).astype(BF16)
        o_ref[rows, :] += jnp.dot(act, wd_ref[...], preferred_element_type=F32)
    carry_g[j] = prev_g
    carry_v[j] = prev_v

    @pl.when(j == pl.num_programs(1) - 1)
    def _():
        x_copy.wait()
        o_ref[...] = x_buf[...] + _rms(o_ref[...], g_ref[...])


def _ffn(h2, x1, w_up, conv_w, conv_b, w_down, g_post, tf, chunks):
    s_len, d = x1.shape
    d_ff = w_down.shape[0]
    nf = d_ff // tf
    tm = sum(chunks)
    assert d_ff % tf == 0 and s_len % tm == 0
    return pl.pallas_call(
        functools.partial(_ffn_kernel, chunks=chunks),
        grid=(s_len // tm, nf),
        in_specs=[
            pl.BlockSpec((tm, d), lambda i, j: (i, 0)),
            pl.BlockSpec((d, tf), lambda i, j: (0, j)),
            pl.BlockSpec((d, tf), lambda i, j: (0, nf + j)),
            pl.BlockSpec((CONV_WIDTH, tf), lambda i, j: (0, j)),
            pl.BlockSpec((CONV_WIDTH, tf), lambda i, j: (0, nf + j)),
            pl.BlockSpec((1, tf), lambda i, j: (0, j)),
            pl.BlockSpec((1, tf), lambda i, j: (0, nf + j)),
            pl.BlockSpec((tf, d), lambda i, j: (j, 0)),
            pl.BlockSpec(memory_space=pl.ANY),
            pl.BlockSpec((1, d), lambda i, j: (0, 0)),
        ],
        out_specs=pl.BlockSpec((tm, d), lambda i, j: (i, 0)),
        out_shape=jax.ShapeDtypeStruct((s_len, d), F32),
        scratch_shapes=[pltpu.VMEM((nf, SUBLANES, tf), F32), pltpu.VMEM((nf, SUBLANES, tf), F32),
                        pltpu.VMEM((tm, d), F32), pltpu.SemaphoreType.DMA(())],
        compiler_params=_params(2),
        name="conv_glu_ffn",
    )(h2, w_up, w_up, conv_w, conv_w, conv_b.reshape(1, -1), conv_b.reshape(1, -1), w_down,
      x1, g_post.reshape(1, d))


def _rel_bucket(dist):
    n = jnp.maximum(dist, 0)
    nf = jnp.maximum(n, 1).astype(F32)
    large = REL_MAX_EXACT + (jnp.log(nf / REL_MAX_EXACT)
                             / math.log(REL_MAX_DISTANCE / REL_MAX_EXACT)
                             * (REL_BUCKETS - REL_MAX_EXACT)).astype(jnp.int32)
    large = jnp.minimum(large, REL_BUCKETS - 1)
    return jnp.where(n < REL_MAX_EXACT, n, large)


def _bucket_blocks(dilation, max_dist, streams=1):
    per = BLOCK // streams
    idx = jnp.arange(BLOCK)
    member = streams * (idx % per) + idx // per
    key = jnp.concatenate([member, BLOCK + member])
    rel = member[:, None] + BLOCK - key[None, :]
    valid = (rel >= 0) & (rel <= max_dist)
    normal = jnp.where(valid, _rel_bucket(rel * dilation), -1)
    first = jnp.where((jnp.arange(2 * BLOCK) >= BLOCK)[None, :], normal, -1)
    return jnp.stack([first, normal]).astype(jnp.int32)


def kernel(x, mem, rel_bias, g_mix_pre, w_in, a_sinks, g_mem, w_mem_kv, g_out, w_out, g_mix_post,
           g_ffn_pre, w_up, conv_w, conv_b, w_down, g_ffn_post):
    bn, s_len, d = x.shape
    assert bn == 1
    depth = w_in.shape[0]
    xs = x.reshape(s_len, d)
    mems = mem.reshape(mem.shape[1], d)

    buckets_a = _bucket_blocks(1, A_WINDOW - 1)
    buckets_b = [_bucket_blocks(dil, window // dil, _streams(dil)) for window, dil in B_PATTERNS]

    off_ka, off_va, off_qb = A_Q, A_Q + A_KV, A_Q + 2 * A_KV
    off_kb, off_vb, off_qm = off_qb + B_W, off_qb + 2 * B_W, off_qb + 3 * B_W
    proj_outs = (((0,), A_Q, Q_SCALE), ((off_ka,), A_KV, 1.0), ((off_va,), A_KV, 1.0),
                 ((off_qb, off_qb + B_QW), B_QW, Q_SCALE), ((off_kb, off_vb), B_W, 1.0),
                 ((off_qm,), M_W, Q_SCALE))
    mem_outs = (((0,), M_W, 1.0), ((M_W,), M_W, 1.0))

    w_in_l = w_in[0].astype(BF16)
    tq = 2048

    for l in range(depth):
        cast = [(w_up, l), (w_down, l), (w_out, l)] + ([(w_in, l + 1)] if l + 1 < depth else [])
        qa, ka, va, qbw, kvbw, qm, w_up_l, w_down_l, w_out_l, *w_in_next = _norm_proj(
            xs, g_mix_pre[l], w_in_l, proj_outs, tm=512, cast=cast)
        w_in_l = w_in_next[0] if w_in_next else None
        mk, mv = _norm_proj(mems, g_mem[l], w_mem_kv[l].astype(BF16), mem_outs, tm=mems.shape[0])
        ya, ym = _attn_am(qa, ka, va, qm, mk, mv, a_sinks[l], rel_bias, buckets_a,
                          g_out[l, :A_Q], g_out[l, A_Q + B_W:], tq=tq)
        obs, lses = [], []
        for (window, dil), buckets in zip(B_PATTERNS, buckets_b):
            if dil == 1:
                o, lse = _attn_b_nat(qbw, kvbw, rel_bias, buckets, tq=tq)
            else:
                o, lse = _attn_b_dil(qbw, kvbw, rel_bias, buckets, dil, tq=min(tq, s_len // dil))
            obs.append(o)
            lses.append(lse)
        x1, h2 = _out_proj(xs, ya, ym, obs, lses, g_out[l, A_Q:A_Q + B_W], w_out_l,
                           g_mix_post[l], g_ffn_pre[l], tm=512, chunk=128)
        xs = _ffn(h2, x1, w_up_l, conv_w[l], conv_b[l], w_down_l, g_ffn_post[l],
                  tf=512, chunks=(640, 384))
    return xs.reshape(bn, s_len, d)
```

```python
import functools
import math

import jax
import jax.numpy as jnp
from jax import lax
from jax.experimental import pallas as pl
from jax.experimental.pallas import tpu as pltpu

F32 = jnp.float32
BF16 = jnp.bfloat16
U32 = jnp.uint32

HEAD_DIM = 128
BLOCK = 128
A_HEADS, A_KV_HEADS, A_GROUP = 6, 2, 3
A_WINDOW = 128
B_HEADS = 6
B_PATTERNS = ((128, 1), (512, 4), (2048, 16))
M_HEADS = 4
REL_BUCKETS, REL_MAX_EXACT, REL_MAX_DISTANCE = 32, 16, 2048
CONV_WIDTH = 3
EPS = 1e-6
NEG = -1e30

A_Q = A_HEADS * HEAD_DIM
A_KV = A_KV_HEADS * HEAD_DIM
B_W = B_HEADS * HEAD_DIM
M_W = M_HEADS * HEAD_DIM
B_QW = B_W // 2
LOG2E = math.log2(math.e)
Q_SCALE = HEAD_DIM ** -0.5 * LOG2E

VMEM_LIMIT_BYTES = 56 * 1024 * 1024
SUBLANES = 8
BF16_ROWS = 16
EPILOGUE_ROWS = 16


def _params(n_axes):
    return pltpu.CompilerParams(dimension_semantics=("arbitrary",) * n_axes,
                                vmem_limit_bytes=VMEM_LIMIT_BYTES)


def _rms(x, g):
    return x * lax.rsqrt(jnp.mean(x * x, axis=-1, keepdims=True) + EPS) * g


def _dot_nt(a, b):
    return lax.dot_general(a, b, (((1,), (1,)), ((), ())), preferred_element_type=F32)


def _softmax_parts(s, sink=None):
    m = jnp.max(s, axis=-1, keepdims=True)
    if sink is not None:
        m = jnp.maximum(m, sink)
    p = jnp.exp2(s - m)
    den = jnp.sum(p, axis=-1, keepdims=True)
    if sink is not None:
        den = den + jnp.exp2(sink - m)
    return p, m, den


def _pack_pair(hi, lo):
    hi_bits = lax.bitcast_convert_type(hi.astype(BF16).astype(F32), U32)
    lo_bits = lax.bitcast_convert_type(lo.astype(BF16).astype(F32), U32)
    return hi_bits | (lo_bits >> 16)


def _unpack_hi(words):
    return lax.bitcast_convert_type(words & jnp.uint32(0xFFFF0000), F32).astype(BF16)


def _unpack_lo(words):
    return lax.bitcast_convert_type(words << 16, F32).astype(BF16)


def _proj_kernel(x_ref, g_ref, w_ref, *refs, outs, n_cast):
    cast_in, o_refs = refs[:n_cast], refs[n_cast:n_cast + len(outs)]
    cast_out = refs[n_cast + len(outs):]
    h = _rms(x_ref[...], g_ref[...]).astype(BF16)

    def cols(off, width, scale):
        acc = jnp.dot(h, w_ref[:, off:off + width], preferred_element_type=F32)
        return acc if scale == 1.0 else acc * scale

    for o_ref, (offs, width, scale) in zip(o_refs, outs):
        if len(offs) == 1:
            o_ref[...] = cols(offs[0], width, scale).astype(BF16)
        else:
            o_ref[...] = _pack_pair(cols(offs[0], width, scale), cols(offs[1], width, scale))

    for src, dst in zip(cast_in, cast_out):
        dst[...] = src[...].astype(BF16)


def _norm_proj(x, g, w, outs, tm, cast=()):
    rows, d = x.shape
    n = w.shape[1]
    steps = rows // tm
    assert rows % tm == 0 and all(a.shape[1] % (BF16_ROWS * steps) == 0 for a, _ in cast)
    return pl.pallas_call(
        functools.partial(_proj_kernel, outs=outs, n_cast=len(cast)),
        grid=(steps,),
        in_specs=[
            pl.BlockSpec((tm, d), lambda i: (i, 0)),
            pl.BlockSpec((1, d), lambda i: (0, 0)),
            pl.BlockSpec((d, n), lambda i: (0, 0), pipeline_mode=pl.Buffered(1)),
        ] + [pl.BlockSpec((None, a.shape[1] // steps, a.shape[2]),
                          functools.partial(lambda layer, i: (layer, i, 0), layer))
             for a, layer in cast],
        out_specs=[pl.BlockSpec((tm, width), lambda i: (i, 0)) for _, width, _ in outs]
        + [pl.BlockSpec((a.shape[1] // steps, a.shape[2]), lambda i: (i, 0)) for a, _ in cast],
        out_shape=[jax.ShapeDtypeStruct((rows, width), BF16 if len(offs) == 1 else U32)
                   for offs, width, _ in outs]
        + [jax.ShapeDtypeStruct(a.shape[1:], BF16) for a, _ in cast],
        compiler_params=_params(1),
        name="norm_proj",
    )(x, g.reshape(1, d), w, *(a for a, _ in cast))


def _lookup_bias(table_ref, bucket, head):
    def body(b, acc):
        return jnp.where(bucket == b, table_ref[b, head] * LOG2E, acc)
    return lax.fori_loop(0, REL_BUCKETS, body, jnp.full(bucket.shape, NEG, F32))


def _attn_am_kernel(sink_ref, table_ref, bucket_ref, qa_ref, ka_ref, kap_ref, va_ref, vap_ref,
                    qm_ref, mk_ref, mv_ref, ga_ref, gm_ref, ya_ref, ym_ref, bias_scr, *, nsub):
    i = pl.program_id(0)

    @pl.when(i == 0)
    def _():
        for var in range(2):
            for h in range(A_HEADS):
                kv, g = divmod(h, A_GROUP)
                bias_scr[var, kv, g * BLOCK:(g + 1) * BLOCK, :] = _lookup_bias(
                    table_ref, bucket_ref[var], h)

    for sb in range(nsub):
        rows = slice(sb * BLOCK, (sb + 1) * BLOCK)
        prev_rows = slice((sb - 1) * BLOCK, sb * BLOCK)
        b_ref = bias_scr.at[jnp.minimum(i, 1) if sb == 0 else 1]
        outs = []
        for kv in range(A_KV_HEADS):
            cols = slice(kv * HEAD_DIM, (kv + 1) * HEAD_DIM)
            if sb == 0:
                k_prev, v_prev = kap_ref[:, cols], vap_ref[:, cols]
            else:
                k_prev, v_prev = ka_ref[prev_rows, cols], va_ref[prev_rows, cols]
            kk = jnp.concatenate([k_prev, ka_ref[rows, cols]], axis=0)
            vv = jnp.concatenate([v_prev, va_ref[rows, cols]], axis=0)
            heads = [kv * A_GROUP + g for g in range(A_GROUP)]
            q3 = jnp.concatenate(
                [qa_ref[rows, h * HEAD_DIM:(h + 1) * HEAD_DIM] for h in heads], axis=0)
            s = _dot_nt(q3, kk) + b_ref[kv]
            sink = jnp.concatenate(
                [jnp.full((BLOCK, 1), sink_ref[h] * LOG2E, F32) for h in heads], axis=0)
            p, _, den = _softmax_parts(s, sink)
            o = jnp.dot(p.astype(BF16), vv, preferred_element_type=F32) / den
            outs += [o[g * BLOCK:(g + 1) * BLOCK] for g in range(A_GROUP)]
        oa = jnp.concatenate(outs, axis=1)
        ya_ref[rows, :] = _rms(oa, ga_ref[...]).astype(BF16)

    outs = []
    for h in range(M_HEADS):
        cols = slice(h * HEAD_DIM, (h + 1) * HEAD_DIM)
        s = _dot_nt(qm_ref[:, cols], mk_ref[:, cols])
        p, _, den = _softmax_parts(s)
        outs.append(jnp.dot(p.astype(BF16), mv_ref[:, cols], preferred_element_type=F32) / den)
    ym_ref[...] = _rms(jnp.concatenate(outs, axis=1), gm_ref[...]).astype(BF16)


def _attn_am(qa, ka, va, qm, mk, mv, sinks, table, buckets, g_a, g_m, tq):
    s_len = qa.shape[0]
    nsub = tq // BLOCK
    mem_len = mk.shape[0]
    prev = lambda i: (jnp.maximum(i * nsub - 1, 0), 0)
    return pl.pallas_call(
        functools.partial(_attn_am_kernel, nsub=nsub),
        grid=(s_len // tq,),
        in_specs=[
            pl.BlockSpec(memory_space=pltpu.SMEM),
            pl.BlockSpec(memory_space=pltpu.SMEM),
            pl.BlockSpec(buckets.shape, lambda i: (0, 0, 0)),
            pl.BlockSpec((tq, A_Q), lambda i: (i, 0)),
            pl.BlockSpec((tq, A_KV), lambda i: (i, 0)),
            pl.BlockSpec((BLOCK, A_KV), prev),
            pl.BlockSpec((tq, A_KV), lambda i: (i, 0)),
            pl.BlockSpec((BLOCK, A_KV), prev),
            pl.BlockSpec((tq, M_W), lambda i: (i, 0)),
            pl.BlockSpec((mem_len, M_W), lambda i: (0, 0)),
            pl.BlockSpec((mem_len, M_W), lambda i: (0, 0)),
            pl.BlockSpec((1, A_Q), lambda i: (0, 0)),
            pl.BlockSpec((1, M_W), lambda i: (0, 0)),
        ],
        out_specs=[pl.BlockSpec((tq, A_Q), lambda i: (i, 0)),
                   pl.BlockSpec((tq, M_W), lambda i: (i, 0))],
        out_shape=[jax.ShapeDtypeStruct((s_len, A_Q), BF16),
                   jax.ShapeDtypeStruct((s_len, M_W), BF16)],
        scratch_shapes=[pltpu.VMEM((2, A_KV_HEADS, A_GROUP * BLOCK, 2 * BLOCK), F32)],
        compiler_params=_params(1),
        name="attn_am",
    )(sinks, table, buckets, qa, ka, ka, va, va, qm, mk, mv,
      g_a.reshape(1, A_Q), g_m.reshape(1, M_W))


def _build_bias_b(table_ref, bucket_ref, bias_scr):
    for var in range(2):
        for h in range(B_HEADS):
            bias_scr[var, h] = _lookup_bias(table_ref, bucket_ref[var], A_HEADS + h)


def _attn_b_tiles(qw_ref, kvw_ref, kvw_prev_ref, bias_scr, first_var, o_ref, lse_ref, nsub):
    ns = qw_ref.shape[0]
    per = BLOCK // ns
    half = B_HEADS // 2
    lane = lax.broadcasted_iota(jnp.int32, (BLOCK, HEAD_DIM), 1)

    def block(ref, sb, cols):
        return jnp.concatenate([ref[st, sb * per:(sb + 1) * per, cols] for st in range(ns)], axis=0)

    unpacked = {}
    for sb in range(nsub):
        b_ref = bias_scr.at[first_var if sb == 0 else 1]
        lse_tile = jnp.zeros((BLOCK, HEAD_DIM), F32)
        for h in range(B_HEADS):
            cols = slice(h * HEAD_DIM, (h + 1) * HEAD_DIM)
            qw = block(qw_ref, sb, slice((h % half) * HEAD_DIM, (h % half + 1) * HEAD_DIM))
            q = _unpack_hi(qw) if h < half else _unpack_lo(qw)
            if sb == 0:
                w_prev = block(kvw_prev_ref, 0, cols)
                k_prev, v_prev = _unpack_hi(w_prev), _unpack_lo(w_prev)
            else:
                k_prev, v_prev = unpacked.pop((sb - 1, h))
            w_cur = block(kvw_ref, sb, cols)
            k_cur, v_cur = _unpack_hi(w_cur), _unpack_lo(w_cur)
            unpacked[sb, h] = (k_cur, v_cur)
            kk = jnp.concatenate([k_prev, k_cur], axis=0)
            vv = jnp.concatenate([v_prev, v_cur], axis=0)
            s = _dot_nt(q, kk) + b_ref[h]
            p, m, den = _softmax_parts(s)
            o = jnp.dot(p.astype(BF16), vv, preferred_element_type=F32) / den
            for st in range(ns):
                o_ref[st, sb * per:(sb + 1) * per, cols] = o[st * per:(st + 1) * per]
            lse_tile = jnp.where(lane == h, m + jnp.log2(den), lse_tile)
        for st in range(ns):
            lse_ref[st, sb * per:(sb + 1) * per, :] = lse_tile[st * per:(st + 1) * per]


def _attn_b_nat_kernel(table_ref, bucket_ref, qw_ref, kvw_ref, kvwp_ref, o_ref, lse_ref, bias_scr,
                       *, nsub):
    i = pl.program_id(0)

    @pl.when(i == 0)
    def _():
        _build_bias_b(table_ref, bucket_ref, bias_scr)

    _attn_b_tiles(qw_ref, kvw_ref, kvwp_ref, bias_scr, jnp.minimum(i, 1), o_ref, lse_ref, nsub)


def _attn_b_dil_kernel(table_ref, bucket_ref, qw_hbm, kvw_hbm, o_hbm, lse_hbm,
                       qbuf, kvbuf, obuf, lbuf, carry, bias_scr, in_sems, out_sems,
                       *, nsub, steps_per_class, dil):
    s, n_steps = pl.program_id(0), pl.num_programs(0)
    ns = qbuf.shape[1]
    rows_per = qbuf.shape[2]
    slot = lax.rem(s, 2)

    def class_rows(step):
        c, i = lax.div(step, steps_per_class), lax.rem(step, steps_per_class)
        return [(pl.ds(i * rows_per, rows_per), c + st * dil) for st in range(ns)]

    def in_copies(step, sl):
        return [pltpu.make_async_copy(hbm.at[rows, col, :], buf.at[sl, st], in_sems.at[a, st, sl])
                for a, (hbm, buf) in enumerate(((qw_hbm, qbuf), (kvw_hbm, kvbuf)))
                for st, (rows, col) in enumerate(class_rows(step))]

    def out_copies(step, sl):
        return [pltpu.make_async_copy(buf.at[sl, st], hbm.at[rows, col, :], out_sems.at[a, st, sl])
                for a, (hbm, buf) in enumerate(((o_hbm, obuf), (lse_hbm, lbuf)))
                for st, (rows, col) in enumerate(class_rows(step))]

    @pl.when(s == 0)
    def _():
        for cp in in_copies(0, 0):
            cp.start()
        _build_bias_b(table_ref, bucket_ref, bias_scr)
        carry[...] = jnp.zeros(carry.shape, U32)

    @pl.when(s + 1 < n_steps)
    def _():
        for cp in in_copies(s + 1, 1 - slot):
            cp.start()

    for cp in in_copies(s, slot):
        cp.wait()

    @pl.when(s >= 2)
    def _():
        for cp in out_copies(s - 2, slot):
            cp.wait()

    first_var = jnp.minimum(lax.rem(s, steps_per_class), 1)
    _attn_b_tiles(qbuf.at[slot], kvbuf.at[slot], carry, bias_scr, first_var,
                  obuf.at[slot], lbuf.at[slot], nsub)
    carry[...] = kvbuf[slot, :, rows_per - carry.shape[1]:, :]

    for cp in out_copies(s, slot):
        cp.start()

    @pl.when(s == n_steps - 1)
    def _():
        for cp in out_copies(s - 1, 1 - slot) + out_copies(s, slot):
            cp.wait()


def _attn_b_nat(qw, kvw, table, buckets, tq):
    s_len = qw.shape[0]
    nsub = tq // BLOCK
    cur = lambda i: (0, i, 0)
    prev = lambda i: (0, jnp.maximum(i * nsub - 1, 0), 0)
    o, lse = pl.pallas_call(
        functools.partial(_attn_b_nat_kernel, nsub=nsub),
        grid=(s_len // tq,),
        in_specs=[
            pl.BlockSpec(memory_space=pltpu.SMEM),
            pl.BlockSpec(buckets.shape, lambda i: (0, 0, 0)),
            pl.BlockSpec((1, tq, B_QW), cur),
            pl.BlockSpec((1, tq, B_W), cur),
            pl.BlockSpec((1, BLOCK, B_W), prev),
        ],
        out_specs=[pl.BlockSpec((1, tq, B_W), cur), pl.BlockSpec((1, tq, HEAD_DIM), cur)],
        out_shape=[jax.ShapeDtypeStruct((1, s_len, B_W), F32),
                   jax.ShapeDtypeStruct((1, s_len, HEAD_DIM), F32)],
        scratch_shapes=[pltpu.VMEM((2, B_HEADS, BLOCK, 2 * BLOCK), F32)],
        compiler_params=_params(1),
        name="attn_b_d1",
    )(table, buckets, qw[None], kvw[None], kvw[None])
    return o[0], lse[0]


def _streams(dil):
    return 1 if dil == 1 else max(1, SUBLANES // dil)


def _attn_b_dil(qw, kvw, table, buckets, dil, tq):
    s_len = qw.shape[0]
    ns = _streams(dil)
    n = dil * ns
    rows_per = tq // ns
    steps_per_class = s_len // dil // tq
    view = lambda t: t.reshape(s_len // n, n, t.shape[-1])
    any_spec = pl.BlockSpec(memory_space=pl.ANY)
    o, lse = pl.pallas_call(
        functools.partial(_attn_b_dil_kernel, nsub=tq // BLOCK, steps_per_class=steps_per_class,
                          dil=dil),
        grid=(dil * steps_per_class,),
        in_specs=[pl.BlockSpec(memory_space=pltpu.SMEM),
                  pl.BlockSpec(buckets.shape, lambda s: (0, 0, 0)), any_spec, any_spec],
        out_specs=[any_spec, any_spec],
        out_shape=[jax.ShapeDtypeStruct((s_len // n, n, B_W), F32),
                   jax.ShapeDtypeStruct((s_len // n, n, HEAD_DIM), F32)],
        scratch_shapes=[
            pltpu.VMEM((2, ns, rows_per, B_QW), U32),
            pltpu.VMEM((2, ns, rows_per, B_W), U32),
            pltpu.VMEM((2, ns, rows_per, B_W), F32),
            pltpu.VMEM((2, ns, rows_per, HEAD_DIM), F32),
            pltpu.VMEM((ns, BLOCK // ns, B_W), U32),
            pltpu.VMEM((2, B_HEADS, BLOCK, 2 * BLOCK), F32),
            pltpu.SemaphoreType.DMA((2, ns, 2)),
            pltpu.SemaphoreType.DMA((2, ns, 2)),
        ],
        compiler_params=_params(1),
        name=f"attn_b_d{dil}",
    )(table, buckets, view(qw), view(kvw))
    return o.reshape(s_len, B_W), lse.reshape(s_len, HEAD_DIM)


def _out_kernel(x_ref, ya_ref, ym_ref, o1_ref, o2_ref, o3_ref, l1_ref, l2_ref, l3_ref,
                gb_ref, w_ref, gpost_ref, gpre_ref, x1_ref, h2_ref, *, chunk):
    o_refs = (o1_ref, o2_ref, o3_ref)
    for c in range(x_ref.shape[0] // chunk):
        rows = slice(c * chunk, (c + 1) * chunk)
        obs = []
        for h in range(B_HEADS):
            cols = slice(h * HEAD_DIM, (h + 1) * HEAD_DIM)
            ls = [l_ref[rows, h:h + 1] for l_ref in (l1_ref, l2_ref, l3_ref)]
            mx = jnp.maximum(jnp.maximum(ls[0], ls[1]), ls[2])
            es = [jnp.exp2(l - mx) for l in ls]
            inv = 1.0 / (es[0] + es[1] + es[2])
            obs.append(sum((e * inv) * o_ref[rows, cols] for e, o_ref in zip(es, o_refs)))
        yb = _rms(jnp.concatenate(obs, axis=1), gb_ref[...]).astype(BF16)
        y_in = jnp.concatenate([ya_ref[rows, :], yb, ym_ref[rows, :]], axis=1)
        y = jnp.dot(y_in, w_ref[...], preferred_element_type=F32)
        x1 = x_ref[rows, :] + _rms(y, gpost_ref[...])
        x1_ref[rows, :] = x1
        h2_ref[rows, :] = _rms(x1, gpre_ref[...]).astype(BF16)


def _out_proj(x, ya, ym, obs, lses, g_b, w_out, g_post, g_pre, tm, chunk):
    s_len, d = x.shape
    row = lambda width: pl.BlockSpec((tm, width), lambda i: (i, 0))
    vec = lambda width: pl.BlockSpec((1, width), lambda i: (0, 0))
    return pl.pallas_call(
        functools.partial(_out_kernel, chunk=chunk),
        grid=(s_len // tm,),
        in_specs=[row(d), row(A_Q), row(M_W), row(B_W), row(B_W), row(B_W),
                  row(HEAD_DIM), row(HEAD_DIM), row(HEAD_DIM), vec(B_W),
                  pl.BlockSpec(w_out.shape, lambda i: (0, 0), pipeline_mode=pl.Buffered(1)),
                  vec(d), vec(d)],
        out_specs=[row(d), row(d)],
        out_shape=[jax.ShapeDtypeStruct((s_len, d), F32), jax.ShapeDtypeStruct((s_len, d), BF16)],
        compiler_params=_params(1),
        name="out_proj",
    )(x, ya, ym, *obs, *lses, g_b.reshape(1, B_W), w_out, g_post.reshape(1, d),
      g_pre.reshape(1, d))


def _shift_rows(u, carry, shift):
    rows, n = u.shape
    above = jnp.concatenate([carry, u[:-SUBLANES]], axis=0).reshape(rows // SUBLANES, SUBLANES, n)
    from_above = lax.broadcasted_iota(jnp.int32, (1, SUBLANES, n), 1) >= SUBLANES - shift
    mixed = jnp.where(from_above, above, u.reshape(rows // SUBLANES, SUBLANES, n))
    return pltpu.roll(mixed, shift, axis=1).reshape(rows, n)


def _causal_conv(u, carry, cw, cb):
    return (cw[0:1] * _shift_rows(u, carry, 2) + cw[1:2] * _shift_rows(u, carry, 1)
            + cw[2:3] * u + cb)


def _ffn_kernel(h_ref, wg_ref, wv_ref, cwg_ref, cwv_ref, cbg_ref, cbv_ref, wd_ref, x_hbm, g_ref,
                o_ref, carry_g, carry_v, x_buf, x_sem, *, chunks):
    i, j = pl.program_id(0), pl.program_id(1)
    tm = x_buf.shape[0]
    x_copy = pltpu.make_async_copy(x_hbm.at[pl.ds(i * tm, tm), :], x_buf, x_sem)

    @pl.when(i == 0)
    def _():
        carry_g[j] = jnp.zeros(carry_g.shape[1:], F32)
        carry_v[j] = jnp.zeros(carry_v.shape[1:], F32)

    @pl.when(j == 0)
    def _():
        x_copy.start()
        o_ref[...] = jnp.zeros(o_ref.shape, F32)

    prev_g, prev_v = carry_g[j], carry_v[j]
    starts = [sum(chunks[:c]) for c in range(len(chunks))]
    for start, size in zip(starts, chunks):
        rows = slice(start, start + size)
        h = h_ref[rows, :]
        ug = jnp.dot(h, wg_ref[...], preferred_element_type=F32)
        uv = jnp.dot(h, wv_ref[...], preferred_element_type=F32)
        cg = _causal_conv(ug, prev_g, cwg_ref[...], cbg_ref[...])
        cv = _causal_conv(uv, prev_v, cwv_ref[...], cbv_ref[...])
        prev_g, prev_v = ug[-SUBLANES:], uv[-SUBLANES:]
        act = (jax.nn.gelu(cg, approximate=True) * cv).astype(BF16)
        o_ref[rows, :] += jnp.dot(act, wd_ref[...], preferred_element_type=F32)
    carry_g[j] = prev_g
    carry_v[j] = prev_v

    @pl.when(j == pl.num_programs(1) - 1)
    def _():
        x_copy.wait()
        for r in range(0, tm, EPILOGUE_ROWS):
            rows = slice(r, r + EPILOGUE_ROWS)
            o_ref[rows, :] = x_buf[rows, :] + _rms(o_ref[rows, :], g_ref[...])


def _ffn(h2, x1, w_up, conv_w, conv_b, w_down, g_post, tf, chunks):
    s_len, d = x1.shape
    d_ff = w_down.shape[0]
    nf = d_ff // tf
    tm = sum(chunks)
    assert d_ff % tf == 0 and s_len % tm == 0
    return pl.pallas_call(
        functools.partial(_ffn_kernel, chunks=chunks),
        grid=(s_len // tm, nf),
        in_specs=[
            pl.BlockSpec((tm, d), lambda i, j: (i, 0)),
            pl.BlockSpec((d, tf), lambda i, j: (0, j)),
            pl.BlockSpec((d, tf), lambda i, j: (0, nf + j)),
            pl.BlockSpec((CONV_WIDTH, tf), lambda i, j: (0, j)),
            pl.BlockSpec((CONV_WIDTH, tf), lambda i, j: (0, nf + j)),
            pl.BlockSpec((1, tf), lambda i, j: (0, j)),
            pl.BlockSpec((1, tf), lambda i, j: (0, nf + j)),
            pl.BlockSpec((tf, d), lambda i, j: (j, 0)),
            pl.BlockSpec(memory_space=pl.ANY),
            pl.BlockSpec((1, d), lambda i, j: (0, 0)),
        ],
        out_specs=pl.BlockSpec((tm, d), lambda i, j: (i, 0)),
        out_shape=jax.ShapeDtypeStruct((s_len, d), F32),
        scratch_shapes=[pltpu.VMEM((nf, SUBLANES, tf), F32), pltpu.VMEM((nf, SUBLANES, tf), F32),
                        pltpu.VMEM((tm, d), F32), pltpu.SemaphoreType.DMA(())],
        compiler_params=_params(2),
        name="conv_glu_ffn",
    )(h2, w_up, w_up, conv_w, conv_w, conv_b.reshape(1, -1), conv_b.reshape(1, -1), w_down,
      x1, g_post.reshape(1, d))


def _rel_bucket(dist):
    n = jnp.maximum(dist, 0)
    nf = jnp.maximum(n, 1).astype(F32)
    large = REL_MAX_EXACT + (jnp.log(nf / REL_MAX_EXACT)
                             / math.log(REL_MAX_DISTANCE / REL_MAX_EXACT)
                             * (REL_BUCKETS - REL_MAX_EXACT)).astype(jnp.int32)
    large = jnp.minimum(large, REL_BUCKETS - 1)
    return jnp.where(n < REL_MAX_EXACT, n, large)


def _bucket_blocks(dilation, max_dist, streams=1):
    per = BLOCK // streams
    idx = jnp.arange(BLOCK)
    member = streams * (idx % per) + idx // per
    key = jnp.concatenate([member, BLOCK + member])
    rel = member[:, None] + BLOCK - key[None, :]
    valid = (rel >= 0) & (rel <= max_dist)
    normal = jnp.where(valid, _rel_bucket(rel * dilation), -1)
    first = jnp.where((jnp.arange(2 * BLOCK) >= BLOCK)[None, :], normal, -1)
    return jnp.stack([first, normal]).astype(jnp.int32)


def kernel(x, mem, rel_bias, g_mix_pre, w_in, a_sinks, g_mem, w_mem_kv, g_out, w_out, g_mix_post,
           g_ffn_pre, w_up, conv_w, conv_b, w_down, g_ffn_post):
    bn, s_len, d = x.shape
    assert bn == 1
    depth = w_in.shape[0]
    xs = x.reshape(s_len, d)
    mems = mem.reshape(mem.shape[1], d)

    buckets_a = _bucket_blocks(1, A_WINDOW - 1)
    buckets_b = [_bucket_blocks(dil, window // dil, _streams(dil)) for window, dil in B_PATTERNS]

    off_ka, off_va, off_qb = A_Q, A_Q + A_KV, A_Q + 2 * A_KV
    off_kb, off_vb, off_qm = off_qb + B_W, off_qb + 2 * B_W, off_qb + 3 * B_W
    proj_outs = (((0,), A_Q, Q_SCALE), ((off_ka,), A_KV, 1.0), ((off_va,), A_KV, 1.0),
                 ((off_qb, off_qb + B_QW), B_QW, Q_SCALE), ((off_kb, off_vb), B_W, 1.0),
                 ((off_qm,), M_W, Q_SCALE))
    mem_outs = (((0,), M_W, 1.0), ((M_W,), M_W, 1.0))

    w_in_l = w_in[0].astype(BF16)
    tq = 2048

    for l in range(depth):
        cast = [(w_up, l), (w_down, l), (w_out, l)] + ([(w_in, l + 1)] if l + 1 < depth else [])
        qa, ka, va, qbw, kvbw, qm, w_up_l, w_down_l, w_out_l, *w_in_next = _norm_proj(
            xs, g_mix_pre[l], w_in_l, proj_outs, tm=512, cast=cast)
        w_in_l = w_in_next[0] if w_in_next else None
        mk, mv = _norm_proj(mems, g_mem[l], w_mem_kv[l].astype(BF16), mem_outs, tm=mems.shape[0])
        ya, ym = _attn_am(qa, ka, va, qm, mk, mv, a_sinks[l], rel_bias, buckets_a,
                          g_out[l, :A_Q], g_out[l, A_Q + B_W:], tq=tq)
        obs, lses = [], []
        for (window, dil), buckets in zip(B_PATTERNS, buckets_b):
            if dil == 1:
                o, lse = _attn_b_nat(qbw, kvbw, rel_bias, buckets, tq=tq)
            else:
                o, lse = _attn_b_dil(qbw, kvbw, rel_bias, buckets, dil, tq=min(tq, s_len // dil))
            obs.append(o)
            lses.append(lse)
        x1, h2 = _out_proj(xs, ya, ym, obs, lses, g_out[l, A_Q:A_Q + B_W], w_out_l,
                           g_mix_post[l], g_ffn_pre[l], tm=512, chunk=128)
        xs = _ffn(h2, x1, w_up_l, conv_w[l], conv_b[l], w_down_l, g_ffn_post[l],
                  tf=512, chunks=(640, 384))
    return xs.reshape(bn, s_len, d)
```
